```python
import math
import jax, jax.numpy as jnp
from jax import lax
import numpy as np

D_MODEL = 1024
BATCH = 16
SEQ = 2048
DEPTH = 4
DEC_BATCH = 2
DEC_SEQ = 8192
PAST_LEN = 128

PLE_DIM = 256
SSD_WIDTH = D_MODEL // 2
SSD_HEAD_DIM = 64
SSD_HEADS = SSD_WIDTH // SSD_HEAD_DIM
SSD_GROUPS = 2
SSD_STATE = 128
SSD_CONV = 3
SSD_CHUNK = 128
S5_WIDTH = D_MODEL // 4
S5_GROUP = 16
S5_GROUPS = S5_WIDTH // S5_GROUP
S5_STATE = 64
SGU_WIDTH = D_MODEL // 4
SGU_HEAD_DIM = 64
SGU_HEADS = SGU_WIDTH // SGU_HEAD_DIM
SGU_CHUNK = 128
D_FF = ((8 * D_MODEL // 3) + 127) // 128 * 128
FFN_CONV = 3
EPS = 1e-6

SSD_XBC = SSD_WIDTH + 2 * SSD_GROUPS * SSD_STATE
O_Z = 0
O_XBC = O_Z + SSD_WIDTH
O_DT = O_XBC + SSD_XBC
O_S5 = O_DT + 2 * SSD_HEADS
O_SGU = O_S5 + S5_WIDTH
IN_WIDTH = O_SGU + 2 * SGU_WIDTH

kernel_name = "hybrid_bidir_ssd_s5_sgu_encoder"

F32 = jnp.float32


def rmsnorm(x, w):
    xf = x.astype(F32)
    y = xf * lax.rsqrt(jnp.mean(xf * xf, axis=-1, keepdims=True) + EPS) * w.astype(F32)
    return y.astype(x.dtype)


def layernorm(x, w, b):
    xf = x.astype(F32)
    mu = jnp.mean(xf, axis=-1, keepdims=True)
    xc = xf - mu
    var = jnp.mean(xc * xc, axis=-1, keepdims=True)
    return xc * lax.rsqrt(var + 1e-5) * w.astype(F32) + b.astype(F32)


def dwconv_centred(x, w, b):
    k_w = w.shape[0]
    pad = k_w // 2
    seq = x.shape[1]
    xp = jnp.pad(x, ((0, 0), (pad, k_w - 1 - pad), (0, 0)))
    out = b
    for k in range(k_w):
        out = out + xp[:, k:k + seq] * w[k]
    return out


def ssd_scan(x, dt, a, b_in, c_in):
    bsz, seq, n_h, p_dim = x.shape
    n_g, n_s = b_in.shape[2], b_in.shape[3]
    n_r = n_h // n_g
    n_c = seq // SSD_CHUNK
    xr = (x * dt[..., None]).reshape(bsz, n_c, SSD_CHUNK, n_g, n_r, p_dim)
    adt = jnp.moveaxis((dt * a).reshape(bsz, n_c, SSD_CHUNK, n_g, n_r), 2, -1)
    cs = jnp.cumsum(adt, axis=-1)
    br = b_in.reshape(bsz, n_c, SSD_CHUNK, n_g, n_s)
    cr = c_in.reshape(bsz, n_c, SSD_CHUNK, n_g, n_s)
    lower = jnp.tril(jnp.ones((SSD_CHUNK, SSD_CHUNK), dtype=bool))
    diff = cs[..., :, None] - cs[..., None, :]
    lmat = jnp.exp(jnp.where(lower, diff, -jnp.inf))
    cb = jnp.einsum('bclgn,bcsgn->bcgls', cr, br)
    y_diag = jnp.einsum('bcgls,bcgrls,bcsgrp->bclgrp', cb, lmat, xr)
    decay = jnp.exp(cs[..., -1:] - cs)
    states = jnp.einsum('bclgn,bcgrl,bclgrp->bcgrpn', br, decay, xr)
    chunk_decay = jnp.exp(cs[..., -1])

    def step(h, inp):
        s, d = inp
        return h * d[..., None, None] + s, h

    h0 = jnp.zeros((bsz, n_g, n_r, p_dim, n_s), F32)
    _, states_in = lax.scan(step, h0, (jnp.moveaxis(states, 1, 0), jnp.moveaxis(chunk_decay, 1, 0)))
    states_in = jnp.moveaxis(states_in, 0, 1)
    y_off = jnp.einsum('bclgn,bcgrpn,bcgrl->bclgrp', cr, states_in, jnp.exp(cs))
    return (y_diag + y_off).reshape(bsz, seq, n_h, p_dim)


def ssd_mixer(z, xbc, dt_raw, conv_w, conv_b, dt_bias, a_log, d_skip, norm_w):
    bsz, seq, _ = z.shape
    xbc = jax.nn.silu(dwconv_centred(xbc.astype(F32), conv_w.astype(F32), conv_b.astype(F32)))
    gn = SSD_GROUPS * SSD_STATE
    xs = xbc[..., :SSD_WIDTH].reshape(bsz, seq, SSD_HEADS, SSD_HEAD_DIM)
    bs = xbc[..., SSD_WIDTH:SSD_WIDTH + gn].reshape(bsz, seq, SSD_GROUPS, SSD_STATE)
    cs = xbc[..., SSD_WIDTH + gn:].reshape(bsz, seq, SSD_GROUPS, SSD_STATE)
    dt = jax.nn.softplus(dt_raw.astype(F32).reshape(bsz, seq, 2, SSD_HEADS) + dt_bias.astype(F32))
    a = -jnp.exp(a_log.astype(F32))
    flip = lambda t: jnp.flip(t, axis=1)
    y_f = ssd_scan(xs, dt[:, :, 0], a[0], bs, cs)
    y_b = flip(ssd_scan(flip(xs), flip(dt[:, :, 1]), a[1], flip(bs), flip(cs)))
    y = (y_f + y_b + d_skip.astype(F32)[:, None] * xs).reshape(bsz, seq, SSD_WIDTH)
    y = y * jax.nn.silu(z.astype(F32))
    return rmsnorm(y, norm_w)


def complex_linear_combine(e1, e2):
    a1r, a1i, b1r, b1i = e1
    a2r, a2i, b2r, b2i = e2
    return (a2r * a1r - a2i * a1i,
            a2r * a1i + a2i * a1r,
            a2r * b1r - a2i * b1i + b2r,
            a2r * b1i + a2i * b1r + b2i)


def s5_mixer(u, lam_re, lam_im, log_step, b_re, b_im, c_re, c_im, d_skip, glu_w, glu_b):
    bsz, seq, width = u.shape
    uf = u.astype(F32)
    ug = uf.reshape(bsz, seq, S5_GROUPS, S5_GROUP)
    br, bi = b_re.astype(F32), b_im.astype(F32)
    cr, ci = c_re.astype(F32), c_im.astype(F32)
    y = uf * d_skip.astype(F32)
    for k, rev in ((0, False), (1, True)):
        step = jnp.exp(log_step[k].astype(F32))[:, None]
        lr, li = lam_re[k].astype(F32), lam_im[k].astype(F32)
        mag = jnp.exp(lr * step)
        ab_re, ab_im = mag * jnp.cos(li * step), mag * jnp.sin(li * step)
        den = lr * lr + li * li
        f_re = ((ab_re - 1.0) * lr + ab_im * li) / den
        f_im = (ab_im * lr - (ab_re - 1.0) * li) / den
        bb_re = f_re[..., None] * br - f_im[..., None] * bi
        bb_im = f_re[..., None] * bi + f_im[..., None] * br
        bu_re = jnp.einsum('blgi,gpi->blgp', ug, bb_re)
        bu_im = jnp.einsum('blgi,gpi->blgp', ug, bb_im)
        a_re = jnp.broadcast_to(ab_re, bu_re.shape)
        a_im = jnp.broadcast_to(ab_im, bu_im.shape)
        _, _, s_re, s_im = lax.associative_scan(
            complex_linear_combine, (a_re, a_im, bu_re, bu_im), reverse=rev, axis=1)
        y_dir = jnp.einsum('blgp,gip->blgi', s_re, cr) - jnp.einsum('blgp,gip->blgi', s_im, ci)
        y = y + y_dir.reshape(bsz, seq, width)
    y = jax.nn.gelu(y)
    return y * jax.nn.sigmoid(y @ glu_w.astype(F32) + glu_b.astype(F32))


def sgu_mixer(uv, norm_w, norm_b, w_s, b_s):
    bsz, seq, _ = uv.shape
    uv = jax.nn.gelu(uv.astype(F32))
    u, v = uv[..., :SGU_WIDTH], uv[..., SGU_WIDTH:]
    v = layernorm(v, norm_w, norm_b)
    vr = v.reshape(bsz, seq // SGU_CHUNK, SGU_CHUNK, SGU_HEADS, SGU_HEAD_DIM)
    mixed = jnp.einsum('hts,bcshd->bcthd', w_s.astype(F32), vr) + jnp.transpose(b_s.astype(F32))[:, :, None]
    return u * mixed.reshape(bsz, seq, SGU_WIDTH)


def encoder(x, p, norm_mix, w_in, ssd_conv_w, ssd_conv_b, ssd_dt_bias, ssd_a_log, ssd_d, ssd_norm,
            s5_lambda_re, s5_lambda_im, s5_log_step, s5_b_re, s5_b_im, s5_c_re, s5_c_im, s5_d,
            s5_glu_w, s5_glu_b, s5_out_norm, sgu_norm_w, sgu_norm_b, sgu_w, sgu_b, sgu_out_norm,
            w_out, norm_ffn, ffn_w_up, ffn_conv_w, ffn_conv_b, ffn_w_down,
            ple_proj, ple_norm, ple_gate_w, final_norm):
    for i in range(DEPTH):
        h = rmsnorm(x, norm_mix[i])
        proj = h @ w_in[i]
        y_ssd = ssd_mixer(proj[..., O_Z:O_XBC], proj[..., O_XBC:O_DT], proj[..., O_DT:O_S5],
                          ssd_conv_w[i], ssd_conv_b[i], ssd_dt_bias[i], ssd_a_log[i], ssd_d[i], ssd_norm[i])
        y_s5 = rmsnorm(s5_mixer(proj[..., O_S5:O_SGU], s5_lambda_re[i], s5_lambda_im[i], s5_log_step[i],
                                s5_b_re[i], s5_b_im[i], s5_c_re[i], s5_c_im[i], s5_d[i],
                                s5_glu_w[i], s5_glu_b[i]), s5_out_norm[i])
        y_sgu = rmsnorm(sgu_mixer(proj[..., O_SGU:], sgu_norm_w[i], sgu_norm_b[i], sgu_w[i], sgu_b[i]),
                        sgu_out_norm[i])
        mix = jnp.concatenate([y_ssd, y_s5, y_sgu], axis=-1).astype(x.dtype)
        x = x + mix @ w_out[i]
        h = rmsnorm(x, norm_ffn[i])
        up = dwconv_centred(h @ ffn_w_up[i], ffn_conv_w[i], ffn_conv_b[i])
        gate, val = up[..., :D_FF], up[..., D_FF:]
        x = x + (jax.nn.silu(gate) * val) @ ffn_w_down[i]
        e = p[i] @ ple_proj[i]
        g = jax.nn.sigmoid(rmsnorm(x, ple_norm[i]) @ ple_gate_w[i])
        x = x + g * e
    return rmsnorm(x, final_norm)


def setup_inputs(seed: int = 0) -> dict:
    key = jax.random.key(seed)
    keys = jax.random.split(key, 64)
    counter = [0]

    def nk():
        counter[0] += 1
        return keys[counter[0] - 1]

    def nrm(shape, scale):
        return scale * jax.random.normal(nk(), shape, F32)

    def gain(shape):
        return 1.0 + nrm(shape, 0.02)

    def log_uniform(shape, lo, hi):
        return jax.random.uniform(nk(), shape, F32, minval=math.log(lo), maxval=math.log(hi))

    L = DEPTH
    ssd_dt = jnp.exp(log_uniform((L, 2, SSD_HEADS), 1e-3, 1e-1))
    lam_im = jnp.broadcast_to(math.pi * jnp.arange(S5_STATE, dtype=F32), (L, 2, S5_GROUPS, S5_STATE))
    return {
        "x_prompt": nrm((BATCH, SEQ, D_MODEL), 1.0),
        "x_sample": nrm((DEC_BATCH, DEC_SEQ, D_MODEL), 1.0),
        "p_prompt": nrm((DEPTH, BATCH, SEQ, PLE_DIM), 1.0),
        "p_sample": nrm((DEPTH, DEC_BATCH, DEC_SEQ, PLE_DIM), 1.0),
        "norm_mix": gain((L, D_MODEL)),
        "w_in": nrm((L, D_MODEL, IN_WIDTH), D_MODEL ** -0.5),
        "ssd_conv_w": nrm((L, SSD_CONV, SSD_XBC), SSD_CONV ** -0.5),
        "ssd_conv_b": nrm((L, SSD_XBC), 0.01),
        "ssd_dt_bias": ssd_dt + jnp.log(-jnp.expm1(-ssd_dt)),
        "ssd_a_log": jnp.log(jax.random.uniform(nk(), (L, 2, SSD_HEADS), F32, minval=1.0, maxval=16.0)),
        "ssd_d": gain((L, SSD_HEADS)),
        "ssd_norm": gain((L, SSD_WIDTH)),
        "s5_lambda_re": -0.5 + nrm((L, 2, S5_GROUPS, S5_STATE), 0.01),
        "s5_lambda_im": lam_im,
        "s5_log_step": log_uniform((L, 2, S5_GROUPS), 1e-3, 1e-1),
        "s5_b_re": nrm((L, S5_GROUPS, S5_STATE, S5_GROUP), (2 * S5_GROUP) ** -0.5),
        "s5_b_im": nrm((L, S5_GROUPS, S5_STATE, S5_GROUP), (2 * S5_GROUP) ** -0.5),
        "s5_c_re": nrm((L, S5_GROUPS, S5_GROUP, S5_STATE), S5_STATE ** -0.5),
        "s5_c_im": nrm((L, S5_GROUPS, S5_GROUP, S5_STATE), S5_STATE ** -0.5),
        "s5_d": nrm((L, S5_WIDTH), 1.0),
        "s5_glu_w": nrm((L, S5_WIDTH, S5_WIDTH), S5_WIDTH ** -0.5),
        "s5_glu_b": nrm((L, S5_WIDTH), 0.01),
        "s5_out_norm": gain((L, S5_WIDTH)),
        "sgu_norm_w": gain((L, SGU_WIDTH)),
        "sgu_norm_b": nrm((L, SGU_WIDTH), 0.01),
        "sgu_w": nrm((L, SGU_HEADS, SGU_CHUNK, SGU_CHUNK), SGU_CHUNK ** -0.5),
        "sgu_b": gain((L, SGU_HEADS, SGU_CHUNK)),
        "sgu_out_norm": gain((L, SGU_WIDTH)),
        "w_out": nrm((L, D_MODEL, D_MODEL), D_MODEL ** -0.5),
        "norm_ffn": gain((L, D_MODEL)),
        "ffn_w_up": nrm((L, D_MODEL, 2 * D_FF), D_MODEL ** -0.5),
        "ffn_conv_w": nrm((L, FFN_CONV, 2 * D_FF), FFN_CONV ** -0.5),
        "ffn_conv_b": nrm((L, 2 * D_FF), 0.01),
        "ffn_w_down": nrm((L, D_FF, D_MODEL), D_FF ** -0.5),
        "ple_proj": nrm((L, PLE_DIM, D_MODEL), PLE_DIM ** -0.5),
        "ple_norm": gain((L, D_MODEL)),
        "ple_gate_w": nrm((L, D_MODEL, D_MODEL), D_MODEL ** -0.5),
        "final_norm": gain((D_MODEL,)),
    }


def reference(x_prompt, x_sample, p_prompt, p_sample, norm_mix, w_in, ssd_conv_w, ssd_conv_b,
              ssd_dt_bias, ssd_a_log, ssd_d, ssd_norm, s5_lambda_re, s5_lambda_im, s5_log_step,
              s5_b_re, s5_b_im, s5_c_re, s5_c_im, s5_d, s5_glu_w, s5_glu_b, s5_out_norm,
              sgu_norm_w, sgu_norm_b, sgu_w, sgu_b, sgu_out_norm, w_out, norm_ffn, ffn_w_up,
              ffn_conv_w, ffn_conv_b, ffn_w_down, ple_proj, ple_norm, ple_gate_w, final_norm):
    weights = (norm_mix, w_in, ssd_conv_w, ssd_conv_b, ssd_dt_bias, ssd_a_log, ssd_d, ssd_norm,
               s5_lambda_re, s5_lambda_im, s5_log_step, s5_b_re, s5_b_im, s5_c_re, s5_c_im, s5_d,
               s5_glu_w, s5_glu_b, s5_out_norm, sgu_norm_w, sgu_norm_b, sgu_w, sgu_b, sgu_out_norm,
               w_out, norm_ffn, ffn_w_up, ffn_conv_w, ffn_conv_b, ffn_w_down,
               ple_proj, ple_norm, ple_gate_w, final_norm)
    y_prompt = encoder(x_prompt, p_prompt, *weights)
    y_sample = encoder(x_sample, p_sample, *weights)
    return (y_prompt, y_sample)
```

```python
import functools
import math

import jax
import jax.numpy as jnp
from jax import lax
from jax.experimental import pallas as pl
from jax.experimental.pallas import tpu as pltpu

F32 = jnp.float32
BF16 = jnp.bfloat16
HIGHEST = lax.Precision.HIGHEST

D_MODEL = 1024
DEPTH = 4
PLE_DIM = 256
SSD_WIDTH = 512
SSD_HEAD_DIM = 64
SSD_HEADS = 8
SSD_GROUPS = 2
SSD_STATE = 128
S5_WIDTH = 256
S5_GROUP = 16
S5_GROUPS = 16
S5_STATE = 64
SGU_WIDTH = 256
SGU_HEAD_DIM = 64
SGU_HEADS = 4
D_FF = 2816
EPS = 1e-6
CHUNK = 128
SSD_XBC = 1024
IN_WIDTH = 2320

P_Z = 0
P_XBC = 512
P_S5 = 1536
P_SGU = 1792
P_DT = 2304
P_WIDTH = 2432

S5_FLAT = S5_GROUP * CHUNK
S5_NSTATE = 4 * S5_STATE
S5_MT_COLS = S5_FLAT + S5_NSTATE

HALO = 8
FF_BLOCK = 256
N_FF_BLOCKS = D_FF // FF_BLOCK

VMEM_LIMIT_BYTES = 56 * 1024 * 1024


def _rmsnorm(x, w):
    return x * lax.rsqrt(jnp.mean(x * x, axis=-1, keepdims=True) + EPS) * w


def _silu(x):
    return x * jax.nn.sigmoid(x)


def _gelu(x):
    return jax.nn.gelu(x)


def _softplus(x):
    return jnp.maximum(x, 0.0) + jnp.log1p(jnp.exp(-jnp.abs(x)))


def _ssd_a_wide(a_log_wide):
    lane = lax.broadcasted_iota(jnp.int32, a_log_wide.shape, 1)
    return jnp.where(lane < 2 * SSD_HEADS, -jnp.exp(a_log_wide), 0.0)


def _expand_heads(v, base, n_heads, width):
    rows = v.shape[0]
    return jnp.concatenate(
        [jnp.broadcast_to(v[:, base + h:base + h + 1], (rows, width)) for h in range(n_heads)],
        axis=1)


def _ssd_chunk(xs, bm, cm, dt_wide, a_wide, state_ref, reverse):
    base = SSD_HEADS if reverse else 0
    adt = dt_wide * a_wide
    row_i = lax.broadcasted_iota(jnp.int32, (CHUNK, CHUNK), 0)
    col_i = lax.broadcasted_iota(jnp.int32, (CHUNK, CHUNK), 1)
    if reverse:
        tri = (col_i >= row_i)
    else:
        tri = (col_i <= row_i)
    cs = jnp.dot(tri.astype(F32), adt, precision=HIGHEST, preferred_element_type=F32)
    cs_t = cs.T
    total = cs[0:1, :] if reverse else cs[CHUNK - 1:CHUNK, :]
    dt_x = _expand_heads(dt_wide, base, SSD_HEADS, SSD_HEAD_DIM)
    dec_x = _expand_heads(jnp.exp(total - cs), base, SSD_HEADS, SSD_HEAD_DIM)
    ecs_x = _expand_heads(jnp.exp(cs), base, SSD_HEADS, SSD_HEAD_DIM)
    cdec_x = _expand_heads(jnp.exp(total), base, SSD_HEADS, SSD_HEAD_DIM)
    xr = xs * dt_x
    xr_b = xr.astype(BF16)
    xrd_b = (xr * dec_x).astype(BF16)
    heads_per_group = SSD_HEADS // SSD_GROUPS
    gw = heads_per_group * SSD_HEAD_DIM
    y_parts = []
    for g in range(SSD_GROUPS):
        b_g = bm[:, g * SSD_STATE:(g + 1) * SSD_STATE].astype(BF16)
        c_g = cm[:, g * SSD_STATE:(g + 1) * SSD_STATE].astype(BF16)
        cb = lax.dot_general(c_g, b_g, (((1,), (1,)), ((), ())), preferred_element_type=F32)
        st = state_ref[g]
        y_off = jnp.dot(c_g, st.astype(BF16), preferred_element_type=F32) * ecs_x[:, g * gw:(g + 1) * gw]
        s_new = lax.dot_general(b_g, xrd_b[:, g * gw:(g + 1) * gw], (((0,), (0,)), ((), ())),
                                preferred_element_type=F32)
        state_ref[g] = st * cdec_x[:, g * gw:(g + 1) * gw] + s_new
        y_d = []
        for r in range(heads_per_group):
            h = g * heads_per_group + r
            diff = cs[:, base + h:base + h + 1] - cs_t[base + h:base + h + 1, :]
            lmat = jnp.exp(jnp.where(tri, diff, -jnp.inf))
            m = (cb * lmat).astype(BF16)
            y_d.append(jnp.dot(m, xr_b[:, h * SSD_HEAD_DIM:(h + 1) * SSD_HEAD_DIM],
                               preferred_element_type=F32))
        y_parts.append(jnp.concatenate(y_d, axis=1) + y_off)
    return jnp.concatenate(y_parts, axis=1)


def _mixer_a_kernel(x_ref, xp_ref, xn_ref, nm_ref, win_ref, cw_ref, cb_ref, dtb_ref, a_ref,
                    sgw_ref, sgb_ref, snw_ref, snb_ref, son_ref,
                    z_ref, xbc_ref, dt_ref, yb_ref, ysgu_ref, ut_ref, state_ref, *, tile):
    i = pl.program_id(1)
    nt = pl.num_programs(1)
    ti = nt - 1 - i
    n_chunks = tile // CHUNK

    x_ext = jnp.concatenate([xp_ref[...], x_ref[...], xn_ref[...]], axis=0)
    h = _rmsnorm(x_ext, nm_ref[...]).astype(BF16)
    proj = jnp.dot(h, win_ref[...], preferred_element_type=F32)

    xbc_e = proj[:, P_XBC:P_XBC + SSD_XBC]
    ext = tile + 2 * HALO
    prev = pltpu.roll(xbc_e, 1, axis=0)[HALO:HALO + tile]
    cur = xbc_e[HALO:HALO + tile]
    nxt = pltpu.roll(xbc_e, ext - 1, axis=0)[HALO:HALO + tile]
    r = lax.broadcasted_iota(jnp.int32, (tile, 1), 0)
    prev = jnp.where(jnp.logical_and(r == 0, ti == 0), 0.0, prev)
    nxt = jnp.where(jnp.logical_and(r == tile - 1, ti == nt - 1), 0.0, nxt)
    cw = cw_ref[...]
    xbc = _silu(cb_ref[...] + cw[0:1] * prev + cw[1:2] * cur + cw[2:3] * nxt)
    xbc_ref[...] = xbc

    pm = proj[HALO:HALO + tile]
    z_ref[...] = pm[:, P_Z:P_Z + SSD_WIDTH]
    dt = _softplus(pm[:, P_DT:P_DT + 128] + dtb_ref[...])
    dt_ref[...] = dt

    uv = _gelu(pm[:, P_SGU:P_SGU + 2 * SGU_WIDTH])
    u = uv[:, :SGU_WIDTH]
    v = uv[:, SGU_WIDTH:]
    mu = jnp.mean(v, axis=-1, keepdims=True)
    vc = v - mu
    var = jnp.mean(vc * vc, axis=-1, keepdims=True)
    v = (vc * lax.rsqrt(var + 1e-5) * snw_ref[...] + snb_ref[...]).astype(BF16)
    mixed = []
    for q in range(n_chunks):
        v_c = v[q * CHUNK:(q + 1) * CHUNK]
        heads = [jnp.dot(sgw_ref[hh], v_c[:, hh * SGU_HEAD_DIM:(hh + 1) * SGU_HEAD_DIM],
                         preferred_element_type=F32) for hh in range(SGU_HEADS)]
        mixed.append(jnp.concatenate(heads, axis=1) + sgb_ref[...])
    mixed = jnp.concatenate(mixed, axis=0)
    ysgu_ref[...] = _rmsnorm(u * mixed, son_ref[...])

    for q in range(n_chunks):
        u5 = pm[q * CHUNK:(q + 1) * CHUNK, P_S5:P_S5 + S5_WIDTH]
        ut_ref[:, q] = u5.T.astype(BF16).reshape(S5_GROUPS, S5_GROUP, CHUNK)

    @pl.when(i == 0)
    def _():
        state_ref[...] = jnp.zeros_like(state_ref)

    a_wide = _ssd_a_wide(a_ref[...])
    for q in reversed(range(n_chunks)):
        sl = slice(q * CHUNK, (q + 1) * CHUNK)
        yb_ref[sl, :] = _ssd_chunk(xbc[sl, 0:512], xbc[sl, 512:768], xbc[sl, 768:1024],
                                   dt[sl], a_wide, state_ref, reverse=True)


def _halo_maps(nt, tile, n_tok, reverse):
    tpb = tile // HALO

    def tidx(b, i):
        return b * nt + ((nt - 1 - i) if reverse else i)

    cur = lambda b, i: (tidx(b, i), 0)
    prev = lambda b, i: (jnp.maximum(tidx(b, i) * tpb - 1, 0), 0)
    nxt = lambda b, i: (jnp.minimum((tidx(b, i) + 1) * tpb, n_tok // HALO - 1), 0)
    return cur, prev, nxt


def _const_spec(arr, layer=None):
    single = pl.Buffered(1)
    if layer is None:
        nd = arr.ndim
        return pl.BlockSpec(arr.shape, lambda b, i, _nd=nd: (0,) * _nd, pipeline_mode=single)
    nd = arr.ndim - 1
    return pl.BlockSpec((None,) + arr.shape[1:], lambda b, i, _nd=nd, _l=layer: (_l,) + (0,) * _nd,
                        pipeline_mode=single)


def _mixer_a(x2d, wts, layer, n_seq, seq_len, tile):
    n_tok = n_seq * seq_len
    nt = seq_len // tile
    nc = tile // CHUNK
    cur, prev, nxt = _halo_maps(nt, tile, n_tok, reverse=True)
    tok = lambda w: pl.BlockSpec((tile, w), cur)
    ut_map = lambda b, i: (0, b * nt + (nt - 1 - i), 0, 0)
    params = [wts["norm_mix"], wts["w_in"], wts["ssd_conv_w"], wts["ssd_conv_b"], wts["dt_bias"],
              wts["ssd_a_log"], wts["sgu_w"], wts["sgu_b"], wts["sgu_norm_w"], wts["sgu_norm_b"],
              wts["sgu_out_norm"]]
    in_specs = [tok(D_MODEL), pl.BlockSpec((HALO, D_MODEL), prev), pl.BlockSpec((HALO, D_MODEL), nxt)]
    in_specs += [_const_spec(p, layer) for p in params]
    out_shape = [
        jax.ShapeDtypeStruct((n_tok, SSD_WIDTH), F32),
        jax.ShapeDtypeStruct((n_tok, SSD_XBC), F32),
        jax.ShapeDtypeStruct((n_tok, 128), F32),
        jax.ShapeDtypeStruct((n_tok, SSD_WIDTH), F32),
        jax.ShapeDtypeStruct((n_tok, SGU_WIDTH), F32),
        jax.ShapeDtypeStruct((S5_GROUPS, n_tok // CHUNK, S5_GROUP, CHUNK), BF16),
    ]
    out_specs = [tok(SSD_WIDTH), tok(SSD_XBC), tok(128), tok(SSD_WIDTH), tok(SGU_WIDTH),
                 pl.BlockSpec((S5_GROUPS, nc, S5_GROUP, CHUNK), ut_map)]
    return pl.pallas_call(
        functools.partial(_mixer_a_kernel, tile=tile),
        grid=(n_seq, nt),
        in_specs=in_specs,
        out_specs=out_specs,
        out_shape=out_shape,
        scratch_shapes=[pltpu.VMEM((SSD_GROUPS, SSD_STATE, 256), F32)],
        compiler_params=pltpu.CompilerParams(
            dimension_semantics=("arbitrary", "arbitrary"), vmem_limit_bytes=VMEM_LIMIT_BYTES),
        name="mixer_a",
    )(x2d, x2d, x2d, *params)


def _cpow(base_re, base_im, n, n_bits):
    shape = n.shape
    res_re = jnp.ones(shape, F32)
    res_im = jnp.zeros(shape, F32)
    b_re = jnp.broadcast_to(base_re, shape)
    b_im = jnp.broadcast_to(base_im, shape)
    for bit in range(n_bits):
        on = ((n >> bit) & 1) == 1
        m_re = res_re * b_re - res_im * b_im
        m_im = res_re * b_im + res_im * b_re
        res_re = jnp.where(on, m_re, res_re)
        res_im = jnp.where(on, m_im, res_im)
        if bit + 1 < n_bits:
            b_re, b_im = b_re * b_re - b_im * b_im, 2.0 * b_re * b_im
    return res_re, res_im


def _s5_discretise(lr, li, st):
    mag = jnp.exp(lr * st)
    a_re = mag * jnp.cos(li * st)
    a_im = mag * jnp.sin(li * st)
    den = lr * lr + li * li
    f_re = ((a_re - 1.0) * lr + a_im * li) / den
    f_im = (a_im * lr - (a_re - 1.0) * li) / den
    return a_re, a_im, f_re, f_im


def _s5_gen_kernel(lrc_ref, lic_ref, lrr_ref, lir_ref, ls_ref, btr_ref, bti_ref, ctr_ref, cti_ref,
                   d_ref, mt_ref, woff_ref, a128_ref, k1_ref, k2_ref, bbar_ref):
    m_lane = lax.broadcasted_iota(jnp.int32, (S5_STATE, CHUNK), 1)
    s_sub = lax.broadcasted_iota(jnp.int32, (CHUNK, S5_STATE), 0)
    ctr = ctr_ref[...]
    cti = cti_ref[...]

    def big_rhs(e_re, e_im):
        blocks = []
        for i in range(S5_GROUP):
            cr = ctr[:, i:i + 1]
            ci = cti[:, i:i + 1]
            blocks.append(jnp.concatenate([cr * e_re - ci * e_im, -(cr * e_im + ci * e_re)], axis=0))
        return jnp.concatenate(blocks, axis=1)

    a_row = []
    offs = []
    ks = []
    for k in range(2):
        st = jnp.exp(ls_ref[k])
        ac_re, ac_im, _, _ = _s5_discretise(lrc_ref[k], lic_ref[k], st)
        ar_re, ar_im, f_re, f_im = _s5_discretise(lrr_ref[k], lir_ref[k], st)
        bt_re = btr_ref[...]
        bt_im = bti_ref[...]
        bb_re = f_re * bt_re - f_im * bt_im
        bb_im = f_re * bt_im + f_im * bt_re
        for j in range(S5_GROUP):
            bbar_ref[2 * k, j] = bb_re[j:j + 1]
            bbar_ref[2 * k + 1, j] = bb_im[j:j + 1]
        a_row.append((ar_re, ar_im))
        if k == 0:
            n_k, n_off = m_lane, m_lane + 1
        else:
            n_k, n_off = (CHUNK - m_lane) & (CHUNK - 1), CHUNK - m_lane
        e_re, e_im = _cpow(ac_re, ac_im, n_k, 7)
        lhs = jnp.concatenate([bb_re, bb_im], axis=1)
        ks.append(jnp.dot(lhs, big_rhs(e_re, e_im), precision=HIGHEST, preferred_element_type=F32))
        o_re, o_im = _cpow(ac_re, ac_im, n_off, 8)
        offs.append(big_rhs(o_re, o_im))

    jj = lax.broadcasted_iota(jnp.int32, (S5_GROUP, S5_FLAT), 0)
    ll = lax.broadcasted_iota(jnp.int32, (S5_GROUP, S5_FLAT), 1)
    k_fwd = ks[0] + jnp.where(ll == jj * CHUNK, d_ref[...], 0.0)
    for j in range(S5_GROUP):
        k1_ref[j] = k_fwd[j:j + 1]
        k2_ref[j] = ks[1][j:j + 1]

    woff_ref[...] = jnp.concatenate(
        [offs[0][0:S5_STATE], offs[1][0:S5_STATE], offs[0][S5_STATE:], offs[1][S5_STATE:]],
        axis=0).astype(BF16)

    t128_f = _cpow(a_row[0][0], a_row[0][1], jnp.full((1, S5_STATE), CHUNK, jnp.int32), 8)
    t128_b = _cpow(a_row[1][0], a_row[1][1], jnp.full((1, S5_STATE), CHUNK, jnp.int32), 8)
    a128_ref[...] = jnp.concatenate([t128_f[0], t128_b[0], t128_f[1], t128_b[1]], axis=1)

    esf = _cpow(a_row[0][0], a_row[0][1], (CHUNK - 1) - s_sub, 7)
    esb = _cpow(a_row[1][0], a_row[1][1], s_sub, 7)
    ss = lax.broadcasted_iota(jnp.int32, (CHUNK, CHUNK), 0)
    tt = lax.broadcasted_iota(jnp.int32, (CHUNK, CHUNK), 1)
    upper = tt >= ss
    lower = tt <= ss

    def per_j(j, carry):
        row = pl.ds(pl.multiple_of(j * CHUNK, CHUNK), CHUNK)
        cols = []
        for k, (e_re, e_im) in enumerate((esf, esb)):
            b_re = bbar_ref[2 * k, j]
            b_im = bbar_ref[2 * k + 1, j]
            cols.append((e_re * b_re - e_im * b_im, e_re * b_im + e_im * b_re))
        mt_ref[row, S5_FLAT:S5_MT_COLS] = jnp.concatenate(
            [cols[0][0], cols[1][0], cols[0][1], cols[1][1]], axis=1).astype(BF16)
        for i in range(S5_GROUP):
            v1 = jnp.broadcast_to(k1_ref[j, :, i * CHUNK:(i + 1) * CHUNK], (CHUNK, CHUNK))
            v2 = jnp.broadcast_to(k2_ref[j, :, i * CHUNK:(i + 1) * CHUNK], (CHUNK, CHUNK))
            r1 = pltpu.roll(v1, 0, axis=1, stride=1, stride_axis=0)
            r2 = pltpu.roll(v2, 0, axis=1, stride=1, stride_axis=0)
            blk = jnp.where(upper, r1, 0.0) + jnp.where(lower, r2, 0.0)
            mt_ref[row, i * CHUNK:(i + 1) * CHUNK] = blk.astype(BF16)
        return carry

    lax.fori_loop(0, S5_GROUP, per_j, 0)


def _s5_gen(wts):
    lg = lambda *blk: pl.BlockSpec((None, 2, None) + blk, lambda l, g: (l, 0, g, 0, 0))
    per = lambda *blk: pl.BlockSpec((None, None) + blk, lambda l, g: (l, g, 0, 0))
    in_specs = [lg(S5_STATE, 1), lg(S5_STATE, 1), lg(1, S5_STATE), lg(1, S5_STATE), lg(1, 1),
                per(S5_GROUP, S5_STATE), per(S5_GROUP, S5_STATE),
                per(S5_STATE, S5_GROUP), per(S5_STATE, S5_GROUP), per(S5_GROUP, 1)]
    out_shape = [jax.ShapeDtypeStruct((DEPTH, S5_GROUPS, S5_FLAT, S5_MT_COLS), BF16),
                 jax.ShapeDtypeStruct((DEPTH, S5_GROUPS, S5_NSTATE, S5_FLAT), BF16),
                 jax.ShapeDtypeStruct((DEPTH, S5_GROUPS, 1, S5_NSTATE), F32)]
    out_specs = [per(S5_FLAT, S5_MT_COLS), per(S5_NSTATE, S5_FLAT), per(1, S5_NSTATE)]
    return pl.pallas_call(
        _s5_gen_kernel,
        grid=(DEPTH, S5_GROUPS),
        in_specs=in_specs,
        out_specs=out_specs,
        out_shape=out_shape,
        scratch_shapes=[pltpu.VMEM((S5_GROUP, 1, S5_FLAT), F32), pltpu.VMEM((S5_GROUP, 1, S5_FLAT), F32),
                        pltpu.VMEM((4, S5_GROUP, 1, S5_STATE), F32)],
        compiler_params=pltpu.CompilerParams(
            dimension_semantics=("arbitrary", "arbitrary"), vmem_limit_bytes=VMEM_LIMIT_BYTES),
        name="s5_gen",
    )(wts["s5_lr_col"], wts["s5_li_col"], wts["s5_lr_row"], wts["s5_li_row"], wts["s5_ls"],
      wts["s5_bt_re"], wts["s5_bt_im"], wts["s5_ct_re"], wts["s5_ct_im"], wts["s5_d"])


def _s5_stage_kernel(ut_ref, mt_ref, woff_ref, a128_ref, yt_ref, sre_ref, sim_ref,
                     fre_ref, fim_ref, bre_ref, bim_ref, *, n_seq, cps):
    z = jnp.dot(ut_ref[...], mt_ref[...], preferred_element_type=F32)
    sre_ref[...] = z[:, S5_FLAT:S5_FLAT + 128]
    sim_ref[...] = z[:, S5_FLAT + 128:S5_FLAT + 256]
    a_re = a128_ref[:, 0:128]
    a_im = a128_ref[:, 128:256]

    def rows(c):
        return pl.ds(c, n_seq, stride=cps) if n_seq > 1 else pl.ds(c, 1)

    def scan(c, carry, in_re_ref, in_im_ref):
        c_re, c_im = carry
        in_re_ref[rows(c), :] = c_re
        in_im_ref[rows(c), :] = c_im
        s_re = sre_ref[rows(c), :]
        s_im = sim_ref[rows(c), :]
        return (a_re * c_re - a_im * c_im + s_re, a_re * c_im + a_im * c_re + s_im)

    zero = (jnp.zeros((n_seq, 128), F32), jnp.zeros((n_seq, 128), F32))
    lax.fori_loop(0, cps, lambda c, cr: scan(c, cr, fre_ref, fim_ref), zero)
    lax.fori_loop(0, cps, lambda c, cr: scan(cps - 1 - c, cr, bre_ref, bim_ref), zero)

    n_rows = n_seq * cps
    lane = lax.broadcasted_iota(jnp.int32, (n_rows, 128), 1)
    fwd_lane = lane < S5_STATE
    sin = jnp.concatenate([jnp.where(fwd_lane, fre_ref[...], bre_ref[...]),
                           jnp.where(fwd_lane, fim_ref[...], bim_ref[...])], axis=1).astype(BF16)
    yt_ref[...] = z[:, 0:S5_FLAT] + jnp.dot(sin, woff_ref[...], preferred_element_type=F32)


def _s5_stage(ut, s5m, layer, n_seq, cps):
    mt, woff, a128 = s5m
    n_chunks = n_seq * cps
    ut3 = ut.reshape(S5_GROUPS, n_chunks, S5_FLAT)
    per = lambda *blk: pl.BlockSpec((None, None) + blk, lambda g, _l=layer: (_l, g, 0, 0))
    grp = pl.BlockSpec((None, n_chunks, S5_FLAT), lambda g: (g, 0, 0))
    scr = pltpu.VMEM((n_chunks, 128), F32)
    yt = pl.pallas_call(
        functools.partial(_s5_stage_kernel, n_seq=n_seq, cps=cps),
        grid=(S5_GROUPS,),
        in_specs=[grp, per(S5_FLAT, S5_MT_COLS), per(S5_NSTATE, S5_FLAT), per(1, S5_NSTATE)],
        out_specs=grp,
        out_shape=jax.ShapeDtypeStruct((S5_GROUPS, n_chunks, S5_FLAT), F32),
        scratch_shapes=[scr] * 6,
        compiler_params=pltpu.CompilerParams(
            dimension_semantics=("arbitrary",), vmem_limit_bytes=VMEM_LIMIT_BYTES),
        name="s5_stage",
    )(ut3, mt, woff, a128)
    return yt.reshape(S5_GROUPS, n_chunks, S5_GROUP, CHUNK)


def _mixer_c_kernel(x_ref, z_ref, xbc_ref, dt_ref, yb_ref, ysgu_ref, yt_ref,
                    a_ref, dx_ref, sn_ref, gw_ref, gb_ref, s5n_ref, wout_ref,
                    o_ref, state_ref, mix_ref, *, tile):
    i = pl.program_id(1)
    n_chunks = tile // CHUNK

    @pl.when(i == 0)
    def _():
        state_ref[...] = jnp.zeros_like(state_ref)

    a_wide = _ssd_a_wide(a_ref[...])
    for q in range(n_chunks):
        sl = slice(q * CHUNK, (q + 1) * CHUNK)
        xs = xbc_ref[sl, 0:512]
        y = _ssd_chunk(xs, xbc_ref[sl, 512:768], xbc_ref[sl, 768:1024], dt_ref[sl, :], a_wide,
                       state_ref, reverse=False)
        y = (y + yb_ref[sl, :] + dx_ref[...] * xs) * _silu(z_ref[sl, :])
        mix_ref[sl, 0:512] = _rmsnorm(y, sn_ref[...]).astype(BF16)

        yt = yt_ref[:, q].reshape(S5_WIDTH, CHUNK).T
        ys = _gelu(yt)
        gate = jnp.dot(ys.astype(BF16), gw_ref[...], preferred_element_type=F32) + gb_ref[...]
        ys = ys * jax.nn.sigmoid(gate)
        mix_ref[sl, 512:768] = _rmsnorm(ys, s5n_ref[...]).astype(BF16)
        mix_ref[sl, 768:1024] = ysgu_ref[sl, :].astype(BF16)

    o_ref[...] = x_ref[...] + jnp.dot(mix_ref[...], wout_ref[...], preferred_element_type=F32)


def _mixer_c(x2d, a_out, yt, wts, layer, n_seq, seq_len, tile):
    z, xbc, dt, yb, ysgu = a_out
    n_tok = n_seq * seq_len
    nt = seq_len // tile
    nc = tile // CHUNK
    cur = lambda b, i: (b * nt + i, 0)
    tok = lambda w: pl.BlockSpec((tile, w), cur)
    params = [wts["ssd_a_log"], wts["ssd_d_x"], wts["ssd_norm"], wts["s5_glu_w"], wts["s5_glu_b"],
              wts["s5_out_norm"], wts["w_out"]]
    in_specs = [tok(D_MODEL), tok(SSD_WIDTH), tok(SSD_XBC), tok(128), tok(SSD_WIDTH), tok(SGU_WIDTH),
                pl.BlockSpec((S5_GROUPS, nc, S5_GROUP, CHUNK), lambda b, i: (0, b * nt + i, 0, 0))]
    in_specs += [_const_spec(p, layer) for p in params]
    return pl.pallas_call(
        functools.partial(_mixer_c_kernel, tile=tile),
        grid=(n_seq, nt),
        in_specs=in_specs,
        out_specs=tok(D_MODEL),
        out_shape=jax.ShapeDtypeStruct((n_tok, D_MODEL), F32),
        scratch_shapes=[pltpu.VMEM((SSD_GROUPS, SSD_STATE, 256), F32),
                        pltpu.VMEM((tile, D_MODEL), BF16)],
        compiler_params=pltpu.CompilerParams(
            dimension_semantics=("arbitrary", "arbitrary"), vmem_limit_bytes=VMEM_LIMIT_BYTES),
        name="mixer_c",
    )(x2d, z, xbc, dt, yb, ysgu, yt, *params)


def _ffn_kernel(x_ref, xp_ref, xn_ref, p_ref, nf_ref, wg_ref, wv_ref, cwg_ref, cwv_ref,
                cbg_ref, cbv_ref, wd_ref, pp_ref, pn_ref, pg_ref, fn_ref, o_ref,
                h_ref, acc_ref, *, tile, final):
    i = pl.program_id(1)
    nt = pl.num_programs(1)
    ext = tile + 2 * HALO
    x = x_ref[...]
    x_ext = jnp.concatenate([xp_ref[...], x, xn_ref[...]], axis=0)
    h_ref[...] = _rmsnorm(x_ext, nf_ref[...]).astype(BF16)
    acc_ref[...] = jnp.zeros_like(acc_ref)
    r = lax.broadcasted_iota(jnp.int32, (tile, 1), 0)
    first_row = jnp.logical_and(r == 0, i == 0)
    last_row = jnp.logical_and(r == tile - 1, i == nt - 1)

    def conv(up, cw, cb):
        prev = jnp.where(first_row, 0.0, pltpu.roll(up, 1, axis=0)[HALO:HALO + tile])
        nxt = jnp.where(last_row, 0.0, pltpu.roll(up, ext - 1, axis=0)[HALO:HALO + tile])
        return cb + cw[0:1] * prev + cw[1:2] * up[HALO:HALO + tile] + cw[2:3] * nxt

    def block(j, carry):
        hh = h_ref[...]
        gate = conv(jnp.dot(hh, wg_ref[j], preferred_element_type=F32), cwg_ref[j], cbg_ref[j])
        val = conv(jnp.dot(hh, wv_ref[j], preferred_element_type=F32), cwv_ref[j], cbv_ref[j])
        act = (_silu(gate) * val).astype(BF16)
        acc_ref[...] += jnp.dot(act, wd_ref[j], preferred_element_type=F32)
        return carry

    lax.fori_loop(0, N_FF_BLOCKS, block, 0)

    x2 = x + acc_ref[...]
    e = jnp.dot(p_ref[...].astype(BF16), pp_ref[...], preferred_element_type=F32)
    g = jax.nn.sigmoid(jnp.dot(_rmsnorm(x2, pn_ref[...]).astype(BF16), pg_ref[...],
                               preferred_element_type=F32))
    x3 = x2 + g * e
    if final:
        x3 = _rmsnorm(x3, fn_ref[...])
    o_ref[...] = x3


def _ffn(x2d, p2d, wts, layer, n_seq, seq_len, tile, final):
    n_tok = n_seq * seq_len
    nt = seq_len // tile
    cur, prev, nxt = _halo_maps(nt, tile, n_tok, reverse=False)
    params = [wts["norm_ffn"], wts["ffn_wg"], wts["ffn_wv"], wts["ffn_cwg"], wts["ffn_cwv"],
              wts["ffn_cbg"], wts["ffn_cbv"], wts["ffn_wd"], wts["ple_proj"], wts["ple_norm"],
              wts["ple_gate_w"]]
    in_specs = [pl.BlockSpec((tile, D_MODEL), cur), pl.BlockSpec((HALO, D_MODEL), prev),
                pl.BlockSpec((HALO, D_MODEL), nxt), pl.BlockSpec((tile, PLE_DIM), cur)]
    in_specs += [_const_spec(p, layer) for p in params]
    in_specs += [_const_spec(wts["final_norm"])]
    return pl.pallas_call(
        functools.partial(_ffn_kernel, tile=tile, final=final),
        grid=(n_seq, nt),
        in_specs=in_specs,
        out_specs=pl.BlockSpec((tile, D_MODEL), cur),
        out_shape=jax.ShapeDtypeStruct((n_tok, D_MODEL), F32),
        scratch_shapes=[pltpu.VMEM((tile + 2 * HALO, D_MODEL), BF16),
                        pltpu.VMEM((tile, D_MODEL), F32)],
        compiler_params=pltpu.CompilerParams(
            dimension_semantics=("arbitrary", "arbitrary"), vmem_limit_bytes=VMEM_LIMIT_BYTES),
        name="ffn",
    )(x2d, x2d, x2d, p2d, *params, wts["final_norm"])


def _prepare_weights(norm_mix, w_in, ssd_conv_w, ssd_conv_b, ssd_dt_bias, ssd_a_log, ssd_d, ssd_norm,
                     s5_lambda_re, s5_lambda_im, s5_log_step, s5_b_re, s5_b_im, s5_c_re, s5_c_im, s5_d,
                     s5_glu_w, s5_glu_b, s5_out_norm, sgu_norm_w, sgu_norm_b, sgu_w, sgu_b, sgu_out_norm,
                     w_out, norm_ffn, ffn_w_up, ffn_conv_w, ffn_conv_b, ffn_w_down,
                     ple_proj, ple_norm, ple_gate_w, final_norm):
    L = DEPTH
    row = lambda a: a.reshape(L, 1, a.shape[-1])
    pad_dt = lambda a: jnp.pad(a.reshape(L, 1, 2 * SSD_HEADS), ((0, 0), (0, 0), (0, 128 - 2 * SSD_HEADS)))
    o_dt = 512 + SSD_XBC
    w_in_p = jnp.concatenate(
        [w_in[..., :o_dt], w_in[..., o_dt + 2 * SSD_HEADS:], w_in[..., o_dt:o_dt + 2 * SSD_HEADS],
         jnp.zeros((L, D_MODEL, P_WIDTH - IN_WIDTH), w_in.dtype)], axis=-1).astype(BF16)
    blk_cols = lambda a: jnp.moveaxis(a.reshape(a.shape[:-1] + (N_FF_BLOCKS, FF_BLOCK)), -2, 1)
    wts = {
        "norm_mix": row(norm_mix),
        "w_in": w_in_p,
        "ssd_conv_w": ssd_conv_w,
        "ssd_conv_b": row(ssd_conv_b),
        "dt_bias": pad_dt(ssd_dt_bias),
        "ssd_a_log": pad_dt(ssd_a_log),
        "ssd_d_x": jnp.repeat(ssd_d, SSD_HEAD_DIM, axis=-1).reshape(L, 1, SSD_WIDTH),
        "ssd_norm": row(ssd_norm),
        "s5_lr_col": s5_lambda_re[..., None],
        "s5_li_col": s5_lambda_im[..., None],
        "s5_lr_row": s5_lambda_re[..., None, :],
        "s5_li_row": s5_lambda_im[..., None, :],
        "s5_ls": s5_log_step[..., None, None],
        "s5_bt_re": jnp.swapaxes(s5_b_re, -1, -2),
        "s5_bt_im": jnp.swapaxes(s5_b_im, -1, -2),
        "s5_ct_re": jnp.swapaxes(s5_c_re, -1, -2),
        "s5_ct_im": jnp.swapaxes(s5_c_im, -1, -2),
        "s5_d": s5_d.reshape(L, S5_GROUPS, S5_GROUP, 1),
        "s5_glu_w": s5_glu_w.astype(BF16),
        "s5_glu_b": row(s5_glu_b),
        "s5_out_norm": row(s5_out_norm),
        "sgu_norm_w": row(sgu_norm_w),
        "sgu_norm_b": row(sgu_norm_b),
        "sgu_w": sgu_w.astype(BF16),
        "sgu_b": jnp.repeat(jnp.swapaxes(sgu_b, -1, -2), SGU_HEAD_DIM, axis=-1),
        "sgu_out_norm": row(sgu_out_norm),
        "w_out": w_out.astype(BF16),
        "norm_ffn": row(norm_ffn),
        "ffn_wg": blk_cols(ffn_w_up[..., :D_FF]).astype(BF16),
        "ffn_wv": blk_cols(ffn_w_up[..., D_FF:]).astype(BF16),
        "ffn_cwg": blk_cols(ffn_conv_w[..., :D_FF]),
        "ffn_cwv": blk_cols(ffn_conv_w[..., D_FF:]),
        "ffn_cbg": blk_cols(ffn_conv_b[:, None, :D_FF]),
        "ffn_cbv": blk_cols(ffn_conv_b[:, None, D_FF:]),
        "ffn_wd": ffn_w_down.reshape(L, N_FF_BLOCKS, FF_BLOCK, D_MODEL).astype(BF16),
        "ple_proj": ple_proj.astype(BF16),
        "ple_norm": row(ple_norm),
        "ple_gate_w": ple_gate_w.astype(BF16),
        "final_norm": final_norm.reshape(1, D_MODEL),
    }
    return wts


def _encoder(x, p, wts, s5m, tile):
    n_seq, seq_len, _ = x.shape
    n_tok = n_seq * seq_len
    x2d = x.reshape(n_tok, D_MODEL)
    for layer in range(DEPTH):
        *a_out, ut = _mixer_a(x2d, wts, layer, n_seq, seq_len, tile)
        yt = _s5_stage(ut, s5m, layer, n_seq, seq_len // CHUNK)
        x2d = _mixer_c(x2d, a_out, yt, wts, layer, n_seq, seq_len, tile)
        x2d = _ffn(x2d, p[layer].reshape(n_tok, PLE_DIM), wts, layer, n_seq, seq_len, tile,
                   final=(layer == DEPTH - 1))
    return x2d.reshape(n_seq, seq_len, D_MODEL)


def kernel(x_prompt, x_sample, p_prompt, p_sample, norm_mix, w_in, ssd_conv_w, ssd_conv_b, ssd_dt_bias, ssd_a_log, ssd_d, ssd_norm, s5_lambda_re, s5_lambda_im, s5_log_step, s5_b_re, s5_b_im, s5_c_re, s5_c_im, s5_d, s5_glu_w, s5_glu_b, s5_out_norm, sgu_norm_w, sgu_norm_b, sgu_w, sgu_b, sgu_out_norm, w_out, norm_ffn, ffn_w_up, ffn_conv_w, ffn_conv_b, ffn_w_down, ple_proj, ple_norm, ple_gate_w, final_norm):
    wts = _prepare_weights(norm_mix, w_in, ssd_conv_w, ssd_conv_b, ssd_dt_bias, ssd_a_log, ssd_d, ssd_norm,
                           s5_lambda_re, s5_lambda_im, s5_log_step, s5_b_re, s5_b_im, s5_c_re, s5_c_im,
                           s5_d, s5_glu_w, s5_glu_b, s5_out_norm, sgu_norm_w, sgu_norm_b, sgu_w, sgu_b,
                           sgu_out_norm, w_out, norm_ffn, ffn_w_up, ffn_conv_w, ffn_conv_b, ffn_w_down,
                           ple_proj, ple_norm, ple_gate_w, final_norm)
    s5m = _s5_gen(wts)
    tile = 512
    y_prompt = _encoder(x_prompt, p_prompt, wts, s5m, tile)
    y_sample = _encoder(x_sample, p_sample, wts, s5m, tile)
    return (y_prompt, y_sample)
```

```python
import functools
import math

import jax
import jax.numpy as jnp
from jax import lax
from jax.experimental import pallas as pl
from jax.experimental.pallas import tpu as pltpu

F32 = jnp.float32
BF16 = jnp.bfloat16
HIGHEST = lax.Precision.HIGHEST

D_MODEL = 1024
DEPTH = 4
PLE_DIM = 256
SSD_WIDTH = 512
SSD_HEAD_DIM = 64
SSD_HEADS = 8
SSD_GROUPS = 2
SSD_STATE = 128
S5_WIDTH = 256
S5_GROUP = 16
S5_GROUPS = 16
S5_STATE = 64
SGU_WIDTH = 256
SGU_HEAD_DIM = 64
SGU_HEADS = 4
D_FF = 2816
EPS = 1e-6
CHUNK = 128
SSD_XBC = 1024
IN_WIDTH = 2320

P_Z = 0
P_XBC = 512
P_S5 = 1536
P_SGU = 1792
P_DT = 2304
P_WIDTH = 2432

S5_FLAT = S5_GROUP * CHUNK
S5_NSTATE = 4 * S5_STATE
S5_MT_COLS = S5_FLAT + S5_NSTATE

HALO = 8
FF_BLOCK = 256
N_FF_BLOCKS = D_FF // FF_BLOCK

VMEM_LIMIT_BYTES = 56 * 1024 * 1024


def _rmsnorm(x, w):
    return x * lax.rsqrt(jnp.mean(x * x, axis=-1, keepdims=True) + EPS) * w


def _silu(x):
    return x * jax.nn.sigmoid(x)


def _gelu(x):
    return jax.nn.gelu(x)


def _softplus(x):
    return jnp.maximum(x, 0.0) + jnp.log1p(jnp.exp(-jnp.abs(x)))


def _split_pair(v):
    lo = lax.broadcasted_iota(jnp.int32, v.shape, 1) < SSD_HEAD_DIM
    return jnp.where(lo, v, 0.0).astype(BF16), jnp.where(lo, 0.0, v).astype(BF16)


def _ssd_tile(xbc_fn, dt, a_log_tiled, state_ref, y_store, *, reverse, n_chunks):
    base = SSD_HEADS if reverse else 0
    row_i = lax.broadcasted_iota(jnp.int32, (CHUNK, CHUNK), 0)
    lane = lax.broadcasted_iota(jnp.int32, (CHUNK, CHUNK), 1)
    tri = (lane >= row_i) if reverse else (lane <= row_i)
    dt_p = jnp.where(lane < 2 * SSD_HEADS, dt[0:CHUNK], 0.0)
    for q in range(1, n_chunks):
        dt_q = jnp.where(lane < 2 * SSD_HEADS, dt[q * CHUNK:(q + 1) * CHUNK], 0.0)
        dt_p = dt_p + pltpu.roll(dt_q, 2 * SSD_HEADS * q, axis=1)
    adt = dt_p * (-jnp.exp(a_log_tiled))
    cs = jnp.dot(tri.astype(F32), adt, precision=HIGHEST, preferred_element_type=F32)
    cs_t = cs.T
    dt_t = dt_p.T
    last = 0 if reverse else CHUNK - 1
    lo_row = lane[0:1] < SSD_HEAD_DIM
    for q in (reversed(range(n_chunks)) if reverse else range(n_chunks)):
        xs, bm, cm = xbc_fn(q)
        ys = []
        for g in range(SSD_GROUPS):
            b_g = bm[:, g * SSD_STATE:(g + 1) * SSD_STATE]
            c_g = cm[:, g * SSD_STATE:(g + 1) * SSD_STATE]
            cb = lax.dot_general(c_g.astype(BF16), b_g.astype(BF16), (((1,), (1,)), ((), ())),
                                 preferred_element_type=F32)
            b_t = b_g.T
            for k in range(2):
                pair = 2 * g + k
                xs_lo, xs_hi = _split_pair(xs[:, pair * CHUNK:(pair + 1) * CHUNK])
                st = state_ref[pair]
                st_lo, st_hi = _split_pair(st)
                lhs_m, lhs_e, bt_w, cdec = [], [], [], []
                for hh in range(2):
                    col = 2 * SSD_HEADS * q + base + 2 * pair + hh
                    cs_col = jnp.broadcast_to(cs[:, col:col + 1], (CHUNK, CHUNK))
                    cs_row = cs_t[col:col + 1, :]
                    dt_row = dt_t[col:col + 1, :]
                    tot = cs_t[col:col + 1, last:last + 1]
                    lmat = jnp.exp(jnp.where(tri, cs_col - cs_row, -jnp.inf))
                    lhs_m.append((cb * lmat * dt_row).astype(BF16))
                    lhs_e.append((c_g * jnp.exp(cs_col)).astype(BF16))
                    bt_w.append((b_t * (jnp.exp(tot - cs_row) * dt_row)).astype(BF16))
                    cdec.append(jnp.exp(tot))
                y_pair = jnp.dot(jnp.concatenate(lhs_m + lhs_e, axis=1),
                                 jnp.concatenate([xs_lo, xs_hi, st_lo, st_hi], axis=0),
                                 preferred_element_type=F32)
                s_new = jnp.dot(jnp.concatenate(bt_w, axis=1), jnp.concatenate([xs_lo, xs_hi], axis=0),
                                preferred_element_type=F32)
                state_ref[pair] = st * jnp.where(lo_row, cdec[0], cdec[1]) + s_new
                ys.append(y_pair)
        y_store(q, jnp.concatenate(ys, axis=1))


def _mixer_a_kernel(x_ref, xp_ref, xn_ref, nm_ref, win_ref, cw_ref, cb_ref, dtb_ref, a_ref,
                    sgw_ref, sgb_ref, snw_ref, snb_ref, son_ref,
                    z_ref, xbc_ref, dt_ref, yb_ref, ysgu_ref, ut_ref, state_ref, *, tile):
    i = pl.program_id(1)
    nt = pl.num_programs(1)
    ti = nt - 1 - i
    n_chunks = tile // CHUNK

    x_ext = jnp.concatenate([xp_ref[...], x_ref[...], xn_ref[...]], axis=0)
    h = _rmsnorm(x_ext, nm_ref[...]).astype(BF16)
    proj = jnp.dot(h, win_ref[...], preferred_element_type=F32)

    xbc_e = proj[:, P_XBC:P_XBC + SSD_XBC]
    ext = tile + 2 * HALO
    prev = pltpu.roll(xbc_e, 1, axis=0)[HALO:HALO + tile]
    cur = xbc_e[HALO:HALO + tile]
    nxt = pltpu.roll(xbc_e, ext - 1, axis=0)[HALO:HALO + tile]
    r = lax.broadcasted_iota(jnp.int32, (tile, 1), 0)
    prev = jnp.where(jnp.logical_and(r == 0, ti == 0), 0.0, prev)
    nxt = jnp.where(jnp.logical_and(r == tile - 1, ti == nt - 1), 0.0, nxt)
    cw = cw_ref[...]
    xbc = _silu(cb_ref[...] + cw[0:1] * prev + cw[1:2] * cur + cw[2:3] * nxt)
    xbc_ref[...] = xbc

    pm = proj[HALO:HALO + tile]
    z_ref[...] = pm[:, P_Z:P_Z + SSD_WIDTH]
    dt = _softplus(pm[:, P_DT:P_DT + 128] + dtb_ref[...])
    dt_ref[...] = dt

    uv = _gelu(pm[:, P_SGU:P_SGU + 2 * SGU_WIDTH])
    u = uv[:, :SGU_WIDTH]
    v = uv[:, SGU_WIDTH:]
    mu = jnp.mean(v, axis=-1, keepdims=True)
    vc = v - mu
    var = jnp.mean(vc * vc, axis=-1, keepdims=True)
    v = vc * lax.rsqrt(var + 1e-5) * snw_ref[...] + snb_ref[...]
    mixed = []
    for q in range(n_chunks):
        pairs = []
        for k in range(SGU_HEADS // 2):
            v_lo, v_hi = _split_pair(v[q * CHUNK:(q + 1) * CHUNK, k * CHUNK:(k + 1) * CHUNK])
            pairs.append(jnp.dot(sgw_ref[k], jnp.concatenate([v_lo, v_hi], axis=0),
                                 preferred_element_type=F32))
        mixed.append(jnp.concatenate(pairs, axis=1) + sgb_ref[...])
    mixed = jnp.concatenate(mixed, axis=0)
    ysgu_ref[...] = _rmsnorm(u * mixed, son_ref[...])

    for q in range(n_chunks):
        u5 = pm[q * CHUNK:(q + 1) * CHUNK, P_S5:P_S5 + S5_WIDTH]
        ut_ref[:, q] = u5.T.astype(BF16).reshape(S5_GROUPS, S5_GROUP, CHUNK)

    @pl.when(i == 0)
    def _():
        state_ref[...] = jnp.zeros_like(state_ref)

    def xbc_fn(q):
        sl = slice(q * CHUNK, (q + 1) * CHUNK)
        return xbc[sl, 0:512], xbc[sl, 512:768], xbc[sl, 768:1024]

    def y_store(q, y):
        yb_ref[q * CHUNK:(q + 1) * CHUNK, :] = y

    _ssd_tile(xbc_fn, dt, a_ref[...], state_ref, y_store, reverse=True, n_chunks=n_chunks)


def _halo_maps(nt, tile, n_tok, reverse):
    tpb = tile // HALO

    def tidx(b, i):
        return b * nt + ((nt - 1 - i) if reverse else i)

    cur = lambda b, i: (tidx(b, i), 0)
    prev = lambda b, i: (jnp.maximum(tidx(b, i) * tpb - 1, 0), 0)
    nxt = lambda b, i: (jnp.minimum((tidx(b, i) + 1) * tpb, n_tok // HALO - 1), 0)
    return cur, prev, nxt


def _const_spec(arr, layer=None):
    single = pl.Buffered(1)
    if layer is None:
        nd = arr.ndim
        return pl.BlockSpec(arr.shape, lambda b, i, _nd=nd: (0,) * _nd, pipeline_mode=single)
    nd = arr.ndim - 1
    return pl.BlockSpec((None,) + arr.shape[1:], lambda b, i, _nd=nd, _l=layer: (_l,) + (0,) * _nd,
                        pipeline_mode=single)


def _mixer_a(x2d, wts, layer, n_seq, seq_len, tile):
    n_tok = n_seq * seq_len
    nt = seq_len // tile
    nc = tile // CHUNK
    cur, prev, nxt = _halo_maps(nt, tile, n_tok, reverse=True)
    tok = lambda w: pl.BlockSpec((tile, w), cur)
    ut_map = lambda b, i: (0, b * nt + (nt - 1 - i), 0, 0)
    params = [wts["norm_mix"], wts["w_in"], wts["ssd_conv_w"], wts["ssd_conv_b"], wts["dt_bias"],
              wts["ssd_a_log"], wts["sgu_w"], wts["sgu_b"], wts["sgu_norm_w"], wts["sgu_norm_b"],
              wts["sgu_out_norm"]]
    in_specs = [tok(D_MODEL), pl.BlockSpec((HALO, D_MODEL), prev), pl.BlockSpec((HALO, D_MODEL), nxt)]
    in_specs += [_const_spec(p, layer) for p in params]
    out_shape = [
        jax.ShapeDtypeStruct((n_tok, SSD_WIDTH), F32),
        jax.ShapeDtypeStruct((n_tok, SSD_XBC), F32),
        jax.ShapeDtypeStruct((n_tok, 128), F32),
        jax.ShapeDtypeStruct((n_tok, SSD_WIDTH), F32),
        jax.ShapeDtypeStruct((n_tok, SGU_WIDTH), F32),
        jax.ShapeDtypeStruct((S5_GROUPS, n_tok // CHUNK, S5_GROUP, CHUNK), BF16),
    ]
    out_specs = [tok(SSD_WIDTH), tok(SSD_XBC), tok(128), tok(SSD_WIDTH), tok(SGU_WIDTH),
                 pl.BlockSpec((S5_GROUPS, nc, S5_GROUP, CHUNK), ut_map)]
    return pl.pallas_call(
        functools.partial(_mixer_a_kernel, tile=tile),
        grid=(n_seq, nt),
        in_specs=in_specs,
        out_specs=out_specs,
        out_shape=out_shape,
        scratch_shapes=[pltpu.VMEM((SSD_HEADS // 2, SSD_STATE, 2 * SSD_HEAD_DIM), F32)],
        compiler_params=pltpu.CompilerParams(
            dimension_semantics=("arbitrary", "arbitrary"), vmem_limit_bytes=VMEM_LIMIT_BYTES),
        name="mixer_a",
    )(x2d, x2d, x2d, *params)


def _cpow(base_re, base_im, n, n_bits):
    shape = n.shape
    res_re = jnp.ones(shape, F32)
    res_im = jnp.zeros(shape, F32)
    b_re = jnp.broadcast_to(base_re, shape)
    b_im = jnp.broadcast_to(base_im, shape)
    for bit in range(n_bits):
        on = ((n >> bit) & 1) == 1
        m_re = res_re * b_re - res_im * b_im
        m_im = res_re * b_im + res_im * b_re
        res_re = jnp.where(on, m_re, res_re)
        res_im = jnp.where(on, m_im, res_im)
        if bit + 1 < n_bits:
            b_re, b_im = b_re * b_re - b_im * b_im, 2.0 * b_re * b_im
    return res_re, res_im


def _s5_discretise(lr, li, st):
    mag = jnp.exp(lr * st)
    a_re = mag * jnp.cos(li * st)
    a_im = mag * jnp.sin(li * st)
    den = lr * lr + li * li
    f_re = ((a_re - 1.0) * lr + a_im * li) / den
    f_im = (a_im * lr - (a_re - 1.0) * li) / den
    return a_re, a_im, f_re, f_im


def _s5_gen_kernel(lrc_ref, lic_ref, lrr_ref, lir_ref, ls_ref, btr_ref, bti_ref, ctr_ref, cti_ref,
                   d_ref, mt_ref, woff_ref, a128_ref, k1_ref, k2_ref, bbar_ref):
    m_lane = lax.broadcasted_iota(jnp.int32, (S5_STATE, CHUNK), 1)
    s_sub = lax.broadcasted_iota(jnp.int32, (CHUNK, S5_STATE), 0)
    ctr = ctr_ref[...]
    cti = cti_ref[...]

    def big_rhs(e_re, e_im):
        blocks = []
        for i in range(S5_GROUP):
            cr = ctr[:, i:i + 1]
            ci = cti[:, i:i + 1]
            blocks.append(jnp.concatenate([cr * e_re - ci * e_im, -(cr * e_im + ci * e_re)], axis=0))
        return jnp.concatenate(blocks, axis=1)

    a_row = []
    offs = []
    ks = []
    for k in range(2):
        st = jnp.exp(ls_ref[k])
        ac_re, ac_im, _, _ = _s5_discretise(lrc_ref[k], lic_ref[k], st)
        ar_re, ar_im, f_re, f_im = _s5_discretise(lrr_ref[k], lir_ref[k], st)
        bt_re = btr_ref[...]
        bt_im = bti_ref[...]
        bb_re = f_re * bt_re - f_im * bt_im
        bb_im = f_re * bt_im + f_im * bt_re
        for j in range(S5_GROUP):
            bbar_ref[2 * k, j] = bb_re[j:j + 1]
            bbar_ref[2 * k + 1, j] = bb_im[j:j + 1]
        a_row.append((ar_re, ar_im))
        if k == 0:
            n_k, n_off = m_lane, m_lane + 1
        else:
            n_k, n_off = (CHUNK - m_lane) & (CHUNK - 1), CHUNK - m_lane
        e_re, e_im = _cpow(ac_re, ac_im, n_k, 7)
        lhs = jnp.concatenate([bb_re, bb_im], axis=1)
        ks.append(jnp.dot(lhs, big_rhs(e_re, e_im), precision=HIGHEST, preferred_element_type=F32))
        o_re, o_im = _cpow(ac_re, ac_im, n_off, 8)
        offs.append(big_rhs(o_re, o_im))

    jj = lax.broadcasted_iota(jnp.int32, (S5_GROUP, S5_FLAT), 0)
    ll = lax.broadcasted_iota(jnp.int32, (S5_GROUP, S5_FLAT), 1)
    k_fwd = ks[0] + jnp.where(ll == jj * CHUNK, d_ref[...], 0.0)
    for j in range(S5_GROUP):
        k1_ref[j] = k_fwd[j:j + 1]
        k2_ref[j] = ks[1][j:j + 1]

    woff_ref[...] = jnp.concatenate(
        [offs[0][0:S5_STATE], offs[1][0:S5_STATE], offs[0][S5_STATE:], offs[1][S5_STATE:]],
        axis=0).astype(BF16)

    t128_f = _cpow(a_row[0][0], a_row[0][1], jnp.full((1, S5_STATE), CHUNK, jnp.int32), 8)
    t128_b = _cpow(a_row[1][0], a_row[1][1], jnp.full((1, S5_STATE), CHUNK, jnp.int32), 8)
    a128_ref[...] = jnp.concatenate([t128_f[0], t128_b[0], t128_f[1], t128_b[1]], axis=1)

    esf = _cpow(a_row[0][0], a_row[0][1], (CHUNK - 1) - s_sub, 7)
    esb = _cpow(a_row[1][0], a_row[1][1], s_sub, 7)
    ss = lax.broadcasted_iota(jnp.int32, (CHUNK, CHUNK), 0)
    tt = lax.broadcasted_iota(jnp.int32, (CHUNK, CHUNK), 1)
    upper = tt >= ss
    lower = tt <= ss

    def per_j(j, carry):
        row = pl.ds(pl.multiple_of(j * CHUNK, CHUNK), CHUNK)
        cols = []
        for k, (e_re, e_im) in enumerate((esf, esb)):
            b_re = bbar_ref[2 * k, j]
            b_im = bbar_ref[2 * k + 1, j]
            cols.append((e_re * b_re - e_im * b_im, e_re * b_im + e_im * b_re))
        mt_ref[row, S5_FLAT:S5_MT_COLS] = jnp.concatenate(
            [cols[0][0], cols[1][0], cols[0][1], cols[1][1]], axis=1).astype(BF16)
        for i in range(S5_GROUP):
            v1 = jnp.broadcast_to(k1_ref[j, :, i * CHUNK:(i + 1) * CHUNK], (CHUNK, CHUNK))
            v2 = jnp.broadcast_to(k2_ref[j, :, i * CHUNK:(i + 1) * CHUNK], (CHUNK, CHUNK))
            r1 = pltpu.roll(v1, 0, axis=1, stride=1, stride_axis=0)
            r2 = pltpu.roll(v2, 0, axis=1, stride=1, stride_axis=0)
            blk = jnp.where(upper, r1, 0.0) + jnp.where(lower, r2, 0.0)
            mt_ref[row, i * CHUNK:(i + 1) * CHUNK] = blk.astype(BF16)
        return carry

    lax.fori_loop(0, S5_GROUP, per_j, 0)


def _s5_gen(wts):
    lg = lambda *blk: pl.BlockSpec((None, 2, None) + blk, lambda l, g: (l, 0, g, 0, 0))
    per = lambda *blk: pl.BlockSpec((None, None) + blk, lambda l, g: (l, g, 0, 0))
    in_specs = [lg(S5_STATE, 1), lg(S5_STATE, 1), lg(1, S5_STATE), lg(1, S5_STATE), lg(1, 1),
                per(S5_GROUP, S5_STATE), per(S5_GROUP, S5_STATE),
                per(S5_STATE, S5_GROUP), per(S5_STATE, S5_GROUP), per(S5_GROUP, 1)]
    out_shape = [jax.ShapeDtypeStruct((DEPTH, S5_GROUPS, S5_FLAT, S5_MT_COLS), BF16),
                 jax.ShapeDtypeStruct((DEPTH, S5_GROUPS, S5_NSTATE, S5_FLAT), BF16),
                 jax.ShapeDtypeStruct((DEPTH, S5_GROUPS, 1, S5_NSTATE), F32)]
    out_specs = [per(S5_FLAT, S5_MT_COLS), per(S5_NSTATE, S5_FLAT), per(1, S5_NSTATE)]
    return pl.pallas_call(
        _s5_gen_kernel,
        grid=(DEPTH, S5_GROUPS),
        in_specs=in_specs,
        out_specs=out_specs,
        out_shape=out_shape,
        scratch_shapes=[pltpu.VMEM((S5_GROUP, 1, S5_FLAT), F32), pltpu.VMEM((S5_GROUP, 1, S5_FLAT), F32),
                        pltpu.VMEM((4, S5_GROUP, 1, S5_STATE), F32)],
        compiler_params=pltpu.CompilerParams(
            dimension_semantics=("arbitrary", "arbitrary"), vmem_limit_bytes=VMEM_LIMIT_BYTES),
        name="s5_gen",
    )(wts["s5_lr_col"], wts["s5_li_col"], wts["s5_lr_row"], wts["s5_li_row"], wts["s5_ls"],
      wts["s5_bt_re"], wts["s5_bt_im"], wts["s5_ct_re"], wts["s5_ct_im"], wts["s5_d"])


def _s5_stage_kernel(ut_ref, mt_ref, woff_ref, a128_ref, yt_ref, sre_ref, sim_ref,
                     fre_ref, fim_ref, bre_ref, bim_ref, *, n_seq, cps):
    z = jnp.dot(ut_ref[...], mt_ref[...], preferred_element_type=F32)
    sre_ref[...] = z[:, S5_FLAT:S5_FLAT + 128]
    sim_ref[...] = z[:, S5_FLAT + 128:S5_FLAT + 256]
    a_re = a128_ref[:, 0:128]
    a_im = a128_ref[:, 128:256]

    def rows(c):
        return pl.ds(c, n_seq, stride=cps) if n_seq > 1 else pl.ds(c, 1)

    def scan(c, carry, in_re_ref, in_im_ref):
        c_re, c_im = carry
        in_re_ref[rows(c), :] = c_re
        in_im_ref[rows(c), :] = c_im
        s_re = sre_ref[rows(c), :]
        s_im = sim_ref[rows(c), :]
        return (a_re * c_re - a_im * c_im + s_re, a_re * c_im + a_im * c_re + s_im)

    zero = (jnp.zeros((n_seq, 128), F32), jnp.zeros((n_seq, 128), F32))
    lax.fori_loop(0, cps, lambda c, cr: scan(c, cr, fre_ref, fim_ref), zero)
    lax.fori_loop(0, cps, lambda c, cr: scan(cps - 1 - c, cr, bre_ref, bim_ref), zero)

    n_rows = n_seq * cps
    lane = lax.broadcasted_iota(jnp.int32, (n_rows, 128), 1)
    fwd_lane = lane < S5_STATE
    sin = jnp.concatenate([jnp.where(fwd_lane, fre_ref[...], bre_ref[...]),
                           jnp.where(fwd_lane, fim_ref[...], bim_ref[...])], axis=1).astype(BF16)
    yt_ref[...] = z[:, 0:S5_FLAT] + jnp.dot(sin, woff_ref[...], preferred_element_type=F32)


def _s5_stage(ut, s5m, layer, n_seq, cps):
    mt, woff, a128 = s5m
    n_chunks = n_seq * cps
    ut3 = ut.reshape(S5_GROUPS, n_chunks, S5_FLAT)
    per = lambda *blk: pl.BlockSpec((None, None) + blk, lambda g, _l=layer: (_l, g, 0, 0))
    grp = pl.BlockSpec((None, n_chunks, S5_FLAT), lambda g: (g, 0, 0))
    scr = pltpu.VMEM((n_chunks, 128), F32)
    yt = pl.pallas_call(
        functools.partial(_s5_stage_kernel, n_seq=n_seq, cps=cps),
        grid=(S5_GROUPS,),
        in_specs=[grp, per(S5_FLAT, S5_MT_COLS), per(S5_NSTATE, S5_FLAT), per(1, S5_NSTATE)],
        out_specs=grp,
        out_shape=jax.ShapeDtypeStruct((S5_GROUPS, n_chunks, S5_FLAT), F32),
        scratch_shapes=[scr] * 6,
        compiler_params=pltpu.CompilerParams(
            dimension_semantics=("arbitrary",), vmem_limit_bytes=VMEM_LIMIT_BYTES),
        name="s5_stage",
    )(ut3, mt, woff, a128)
    return yt.reshape(S5_GROUPS, n_chunks, S5_GROUP, CHUNK)


def _mixer_c_kernel(x_ref, z_ref, xbc_ref, dt_ref, yb_ref, ysgu_ref, yt_ref,
                    a_ref, dx_ref, sn_ref, gw_ref, gb_ref, s5n_ref, wout_ref,
                    o_ref, state_ref, mix_ref, *, tile):
    i = pl.program_id(1)
    n_chunks = tile // CHUNK

    @pl.when(i == 0)
    def _():
        state_ref[...] = jnp.zeros_like(state_ref)

    def xbc_fn(q):
        sl = slice(q * CHUNK, (q + 1) * CHUNK)
        return xbc_ref[sl, 0:512], xbc_ref[sl, 512:768], xbc_ref[sl, 768:1024]

    def y_store(q, y):
        sl = slice(q * CHUNK, (q + 1) * CHUNK)
        y = (y + yb_ref[sl, :] + dx_ref[...] * xbc_ref[sl, 0:512]) * _silu(z_ref[sl, :])
        mix_ref[sl, 0:512] = _rmsnorm(y, sn_ref[...]).astype(BF16)

    _ssd_tile(xbc_fn, dt_ref[...], a_ref[...], state_ref, y_store, reverse=False, n_chunks=n_chunks)

    for q in range(n_chunks):
        sl = slice(q * CHUNK, (q + 1) * CHUNK)
        yt = yt_ref[:, q].reshape(S5_WIDTH, CHUNK).T
        ys = _gelu(yt)
        gate = jnp.dot(ys.astype(BF16), gw_ref[...], preferred_element_type=F32) + gb_ref[...]
        ys = ys * jax.nn.sigmoid(gate)
        mix_ref[sl, 512:768] = _rmsnorm(ys, s5n_ref[...]).astype(BF16)
        mix_ref[sl, 768:1024] = ysgu_ref[sl, :].astype(BF16)

    o_ref[...] = x_ref[...] + jnp.dot(mix_ref[...], wout_ref[...], preferred_element_type=F32)


def _mixer_c(x2d, a_out, yt, wts, layer, n_seq, seq_len, tile):
    z, xbc, dt, yb, ysgu = a_out
    n_tok = n_seq * seq_len
    nt = seq_len // tile
    nc = tile // CHUNK
    cur = lambda b, i: (b * nt + i, 0)
    tok = lambda w: pl.BlockSpec((tile, w), cur)
    params = [wts["ssd_a_log"], wts["ssd_d_x"], wts["ssd_norm"], wts["s5_glu_w"], wts["s5_glu_b"],
              wts["s5_out_norm"], wts["w_out"]]
    in_specs = [tok(D_MODEL), tok(SSD_WIDTH), tok(SSD_XBC), tok(128), tok(SSD_WIDTH), tok(SGU_WIDTH),
                pl.BlockSpec((S5_GROUPS, nc, S5_GROUP, CHUNK), lambda b, i: (0, b * nt + i, 0, 0))]
    in_specs += [_const_spec(p, layer) for p in params]
    return pl.pallas_call(
        functools.partial(_mixer_c_kernel, tile=tile),
        grid=(n_seq, nt),
        in_specs=in_specs,
        out_specs=tok(D_MODEL),
        out_shape=jax.ShapeDtypeStruct((n_tok, D_MODEL), F32),
        scratch_shapes=[pltpu.VMEM((SSD_HEADS // 2, SSD_STATE, 2 * SSD_HEAD_DIM), F32),
                        pltpu.VMEM((tile, D_MODEL), BF16)],
        compiler_params=pltpu.CompilerParams(
            dimension_semantics=("arbitrary", "arbitrary"), vmem_limit_bytes=VMEM_LIMIT_BYTES),
        name="mixer_c",
    )(x2d, z, xbc, dt, yb, ysgu, yt, *params)


def _ffn_kernel(x_ref, xp_ref, xn_ref, p_ref, nf_ref, wg_ref, wv_ref, cwg_ref, cwv_ref,
                cbg_ref, cbv_ref, wd_ref, pp_ref, pn_ref, pg_ref, fn_ref, o_ref,
                act_ref, *, tile, final):
    i = pl.program_id(1)
    nt = pl.num_programs(1)
    ext = tile + 2 * HALO
    x = x_ref[...]
    x_ext = jnp.concatenate([xp_ref[...], x, xn_ref[...]], axis=0)
    hh = _rmsnorm(x_ext, nf_ref[...]).astype(BF16)
    r = lax.broadcasted_iota(jnp.int32, (tile, 1), 0)
    first_row = jnp.logical_and(r == 0, i == 0)
    last_row = jnp.logical_and(r == tile - 1, i == nt - 1)

    def conv(up, cw, cb):
        prev = jnp.where(first_row, 0.0, pltpu.roll(up, 1, axis=0)[HALO:HALO + tile])
        nxt = jnp.where(last_row, 0.0, pltpu.roll(up, ext - 1, axis=0)[HALO:HALO + tile])
        return cb + cw[0:1] * prev + cw[1:2] * up[HALO:HALO + tile] + cw[2:3] * nxt

    for j in range(N_FF_BLOCKS):
        gate = conv(jnp.dot(hh, wg_ref[j], preferred_element_type=F32), cwg_ref[j], cbg_ref[j])
        val = conv(jnp.dot(hh, wv_ref[j], preferred_element_type=F32), cwv_ref[j], cbv_ref[j])
        act_ref[:, j * FF_BLOCK:(j + 1) * FF_BLOCK] = (_silu(gate) * val).astype(BF16)

    x2 = x + jnp.dot(act_ref[...], wd_ref[...], preferred_element_type=F32)
    e = jnp.dot(p_ref[...].astype(BF16), pp_ref[...], preferred_element_type=F32)
    g = jax.nn.sigmoid(jnp.dot(_rmsnorm(x2, pn_ref[...]).astype(BF16), pg_ref[...],
                               preferred_element_type=F32))
    x3 = x2 + g * e
    if final:
        x3 = _rmsnorm(x3, fn_ref[...])
    o_ref[...] = x3


def _ffn(x2d, p2d, wts, layer, n_seq, seq_len, tile, final):
    n_tok = n_seq * seq_len
    nt = seq_len // tile
    cur, prev, nxt = _halo_maps(nt, tile, n_tok, reverse=False)
    params = [wts["norm_ffn"], wts["ffn_wg"], wts["ffn_wv"], wts["ffn_cwg"], wts["ffn_cwv"],
              wts["ffn_cbg"], wts["ffn_cbv"], wts["ffn_wd"], wts["ple_proj"], wts["ple_norm"],
              wts["ple_gate_w"]]
    in_specs = [pl.BlockSpec((tile, D_MODEL), cur), pl.BlockSpec((HALO, D_MODEL), prev),
                pl.BlockSpec((HALO, D_MODEL), nxt), pl.BlockSpec((tile, PLE_DIM), cur)]
    in_specs += [_const_spec(p, layer) for p in params]
    in_specs += [_const_spec(wts["final_norm"])]
    return pl.pallas_call(
        functools.partial(_ffn_kernel, tile=tile, final=final),
        grid=(n_seq, nt),
        in_specs=in_specs,
        out_specs=pl.BlockSpec((tile, D_MODEL), cur),
        out_shape=jax.ShapeDtypeStruct((n_tok, D_MODEL), F32),
        scratch_shapes=[pltpu.VMEM((tile, D_FF), BF16)],
        compiler_params=pltpu.CompilerParams(
            dimension_semantics=("arbitrary", "arbitrary"), vmem_limit_bytes=VMEM_LIMIT_BYTES),
        name="ffn",
    )(x2d, x2d, x2d, p2d, *params, wts["final_norm"])


def _prepare_weights(norm_mix, w_in, ssd_conv_w, ssd_conv_b, ssd_dt_bias, ssd_a_log, ssd_d, ssd_norm,
                     s5_lambda_re, s5_lambda_im, s5_log_step, s5_b_re, s5_b_im, s5_c_re, s5_c_im, s5_d,
                     s5_glu_w, s5_glu_b, s5_out_norm, sgu_norm_w, sgu_norm_b, sgu_w, sgu_b, sgu_out_norm,
                     w_out, norm_ffn, ffn_w_up, ffn_conv_w, ffn_conv_b, ffn_w_down,
                     ple_proj, ple_norm, ple_gate_w, final_norm):
    L = DEPTH
    row = lambda a: a.reshape(L, 1, a.shape[-1])
    pad_dt = lambda a: jnp.pad(a.reshape(L, 1, 2 * SSD_HEADS), ((0, 0), (0, 0), (0, 128 - 2 * SSD_HEADS)))
    o_dt = 512 + SSD_XBC
    w_in_p = jnp.concatenate(
        [w_in[..., :o_dt], w_in[..., o_dt + 2 * SSD_HEADS:], w_in[..., o_dt:o_dt + 2 * SSD_HEADS],
         jnp.zeros((L, D_MODEL, P_WIDTH - IN_WIDTH), w_in.dtype)], axis=-1).astype(BF16)
    blk_cols = lambda a: jnp.moveaxis(a.reshape(a.shape[:-1] + (N_FF_BLOCKS, FF_BLOCK)), -2, 1)
    wts = {
        "norm_mix": row(norm_mix),
        "w_in": w_in_p,
        "ssd_conv_w": ssd_conv_w,
        "ssd_conv_b": row(ssd_conv_b),
        "dt_bias": pad_dt(ssd_dt_bias),
        "ssd_a_log": jnp.tile(ssd_a_log.reshape(L, 1, 2 * SSD_HEADS), (1, 1, 128 // (2 * SSD_HEADS))),
        "ssd_d_x": jnp.repeat(ssd_d, SSD_HEAD_DIM, axis=-1).reshape(L, 1, SSD_WIDTH),
        "ssd_norm": row(ssd_norm),
        "s5_lr_col": s5_lambda_re[..., None],
        "s5_li_col": s5_lambda_im[..., None],
        "s5_lr_row": s5_lambda_re[..., None, :],
        "s5_li_row": s5_lambda_im[..., None, :],
        "s5_ls": s5_log_step[..., None, None],
        "s5_bt_re": jnp.swapaxes(s5_b_re, -1, -2),
        "s5_bt_im": jnp.swapaxes(s5_b_im, -1, -2),
        "s5_ct_re": jnp.swapaxes(s5_c_re, -1, -2),
        "s5_ct_im": jnp.swapaxes(s5_c_im, -1, -2),
        "s5_d": s5_d.reshape(L, S5_GROUPS, S5_GROUP, 1),
        "s5_glu_w": s5_glu_w.astype(BF16),
        "s5_glu_b": row(s5_glu_b),
        "s5_out_norm": row(s5_out_norm),
        "sgu_norm_w": row(sgu_norm_w),
        "sgu_norm_b": row(sgu_norm_b),
        "sgu_w": jnp.concatenate([sgu_w[:, 0::2], sgu_w[:, 1::2]], axis=-1).astype(BF16),
        "sgu_b": jnp.repeat(jnp.swapaxes(sgu_b, -1, -2), SGU_HEAD_DIM, axis=-1),
        "sgu_out_norm": row(sgu_out_norm),
        "w_out": w_out.astype(BF16),
        "norm_ffn": row(norm_ffn),
        "ffn_wg": blk_cols(ffn_w_up[..., :D_FF]).astype(BF16),
        "ffn_wv": blk_cols(ffn_w_up[..., D_FF:]).astype(BF16),
        "ffn_cwg": blk_cols(ffn_conv_w[..., :D_FF]),
        "ffn_cwv": blk_cols(ffn_conv_w[..., D_FF:]),
        "ffn_cbg": blk_cols(ffn_conv_b[:, None, :D_FF]),
        "ffn_cbv": blk_cols(ffn_conv_b[:, None, D_FF:]),
        "ffn_wd": ffn_w_down.astype(BF16),
        "ple_proj": ple_proj.astype(BF16),
        "ple_norm": row(ple_norm),
        "ple_gate_w": ple_gate_w.astype(BF16),
        "final_norm": final_norm.reshape(1, D_MODEL),
    }
    return wts


def _encoder(x, p, wts, s5m, tile):
    n_seq, seq_len, _ = x.shape
    n_tok = n_seq * seq_len
    x2d = x.reshape(n_tok, D_MODEL)
    for layer in range(DEPTH):
        *a_out, ut = _mixer_a(x2d, wts, layer, n_seq, seq_len, tile)
        yt = _s5_stage(ut, s5m, layer, n_seq, seq_len // CHUNK)
        x2d = _mixer_c(x2d, a_out, yt, wts, layer, n_seq, seq_len, tile)
        x2d = _ffn(x2d, p[layer].reshape(n_tok, PLE_DIM), wts, layer, n_seq, seq_len, tile,
                   final=(layer == DEPTH - 1))
    return x2d.reshape(n_seq, seq_len, D_MODEL)


def kernel(x_prompt, x_sample, p_prompt, p_sample, norm_mix, w_in, ssd_conv_w, ssd_conv_b, ssd_dt_bias, ssd_a_log, ssd_d, ssd_norm, s5_lambda_re, s5_lambda_im, s5_log_step, s5_b_re, s5_b_im, s5_c_re, s5_c_im, s5_d, s5_glu_w, s5_glu_b, s5_out_norm, sgu_norm_w, sgu_norm_b, sgu_w, sgu_b, sgu_out_norm, w_out, norm_ffn, ffn_w_up, ffn_conv_w, ffn_conv_b, ffn_w_down, ple_proj, ple_norm, ple_gate_w, final_norm):
    wts = _prepare_weights(norm_mix, w_in, ssd_conv_w, ssd_conv_b, ssd_dt_bias, ssd_a_log, ssd_d, ssd_norm,
                           s5_lambda_re, s5_lambda_im, s5_log_step, s5_b_re, s5_b_im, s5_c_re, s5_c_im,
                           s5_d, s5_glu_w, s5_glu_b, s5_out_norm, sgu_norm_w, sgu_norm_b, sgu_w, sgu_b,
                           sgu_out_norm, w_out, norm_ffn, ffn_w_up, ffn_conv_w, ffn_conv_b, ffn_w_down,
                           ple_proj, ple_norm, ple_gate_w, final_norm)
    s5m = _s5_gen(wts)
    tile = 512
    y_prompt = _encoder(x_prompt, p_prompt, wts, s5m, tile)
    y_sample = _encoder(x_sample, p_sample, wts, s5m, tile)
    return (y_prompt, y_sample)
```

```python
import functools

import jax
import jax.numpy as jnp
from jax import lax
from jax.experimental import pallas as pl
from jax.experimental.pallas import tpu as pltpu

F32 = jnp.float32
BF16 = jnp.bfloat16
HIGHEST = lax.Precision.HIGHEST

D_MODEL = 1024
DEPTH = 4
PLE_DIM = 256
SSD_WIDTH = 512
SSD_HEAD_DIM = 64
SSD_HEADS = 8
SSD_GROUPS = 2
SSD_STATE = 128
S5_WIDTH = 256
S5_GROUP = 16
S5_GROUPS = 16
S5_STATE = 64
SGU_WIDTH = 256
SGU_HEAD_DIM = 64
SGU_HEADS = 4
D_FF = 2816
EPS = 1e-6
CHUNK = 128
SSD_XBC = 1024
IN_WIDTH = 2320

P_Z = 0
P_XBC = 512
P_S5 = 1536
P_SGU = 1792
P_DT = 2304
P_WIDTH = 2432

S5_FLAT = S5_GROUP * CHUNK
S5_NSTATE = 4 * S5_STATE
S5_MT_COLS = S5_FLAT + S5_NSTATE
TOE_ROWS = 16
TOE_SHIFTS = CHUNK // TOE_ROWS

MIX_TILE = 1024
MIX_SUB = 256
MIX_HALO = 16
FFN_TILE = 512
FFN_HALO = 8
FF_BLOCK = 256
N_FF_BLOCKS = D_FF // FF_BLOCK

VMEM_LIMIT_BYTES = 56 * 1024 * 1024


def _rmsnorm(x, w):
    return x * lax.rsqrt(jnp.mean(x * x, axis=-1, keepdims=True) + EPS) * w


def _silu(x):
    return x * jax.nn.sigmoid(x)


def _gelu(x):
    return jax.nn.gelu(x)


def _softplus(x):
    return jnp.maximum(x, 0.0) + jnp.log1p(jnp.exp(-jnp.abs(x)))


def _split_pair(v):
    lo = lax.broadcasted_iota(jnp.int32, v.shape, 1) < SSD_HEAD_DIM
    return jnp.where(lo, v, 0.0).astype(BF16), jnp.where(lo, 0.0, v).astype(BF16)


def _ssd_prep(dt, a_log_tiled, *, reverse, n_chunks):
    row_i = lax.broadcasted_iota(jnp.int32, (CHUNK, CHUNK), 0)
    lane = lax.broadcasted_iota(jnp.int32, (CHUNK, CHUNK), 1)
    tri = (lane >= row_i) if reverse else (lane <= row_i)
    dt_p = jnp.where(lane < 2 * SSD_HEADS, dt[0:CHUNK], 0.0)
    for q in range(1, n_chunks):
        dt_q = jnp.where(lane < 2 * SSD_HEADS, dt[q * CHUNK:(q + 1) * CHUNK], 0.0)
        dt_p = dt_p + pltpu.roll(dt_q, 2 * SSD_HEADS * q, axis=1)
    adt = dt_p * (-jnp.exp(a_log_tiled))
    cs = jnp.dot(tri.astype(F32), adt, precision=HIGHEST, preferred_element_type=F32)
    return tri, cs, cs.T, dt_p.T


def _ssd_chunk(q, xs, bm, cm, prep, state_ref, *, reverse):
    tri, cs, cs_t, dt_t = prep
    base = SSD_HEADS if reverse else 0
    last = 0 if reverse else CHUNK - 1
    lo_row = lax.broadcasted_iota(jnp.int32, (1, CHUNK), 1) < SSD_HEAD_DIM
    ys = []
    for g in range(SSD_GROUPS):
        b_g = bm[:, g * SSD_STATE:(g + 1) * SSD_STATE]
        c_g = cm[:, g * SSD_STATE:(g + 1) * SSD_STATE]
        cb = lax.dot_general(c_g.astype(BF16), b_g.astype(BF16), (((1,), (1,)), ((), ())),
                             preferred_element_type=F32)
        b_t = b_g.T
        for k in range(2):
            pair = 2 * g + k
            xs_lo, xs_hi = _split_pair(xs[:, pair * CHUNK:(pair + 1) * CHUNK])
            st = state_ref[pair]
            st_lo, st_hi = _split_pair(st)
            lhs_m, lhs_e, bt_w, cdec = [], [], [], []
            for hh in range(2):
                col = 2 * SSD_HEADS * q + base + 2 * pair + hh
                cs_col = jnp.broadcast_to(cs[:, col:col + 1], (CHUNK, CHUNK))
                cs_row = cs_t[col:col + 1, :]
                dt_row = dt_t[col:col + 1, :]
                tot = cs_t[col:col + 1, last:last + 1]
                lmat = jnp.exp(jnp.where(tri, cs_col - cs_row, -jnp.inf))
                lhs_m.append((cb * lmat * dt_row).astype(BF16))
                lhs_e.append((c_g * jnp.exp(cs_col)).astype(BF16))
                bt_w.append((b_t * (jnp.exp(tot - cs_row) * dt_row)).astype(BF16))
                cdec.append(jnp.exp(tot))
            y_pair = jnp.dot(jnp.concatenate(lhs_m + lhs_e, axis=1),
                             jnp.concatenate([xs_lo, xs_hi, st_lo, st_hi], axis=0),
                             preferred_element_type=F32)
            s_new = jnp.dot(jnp.concatenate(bt_w, axis=1), jnp.concatenate([xs_lo, xs_hi], axis=0),
                            preferred_element_type=F32)
            state_ref[pair] = st * jnp.where(lo_row, cdec[0], cdec[1]) + s_new
            ys.append(y_pair)
    return jnp.concatenate(ys, axis=1)


def _mixer_a_kernel(x_ref, xp_ref, xn_ref, nm_ref, win_ref, cw_ref, cb_ref, dtb_ref, a_ref,
                    sgw_ref, sgb_ref, snw_ref, snb_ref, son_ref,
                    z_ref, xbc_ref, dt_ref, yb_ref, ysgu_ref, ut_ref,
                    state_ref, h_ref, xe_ref, *, tile):
    i = pl.program_id(1)
    nt = pl.num_programs(1)
    ti = nt - 1 - i
    n_chunks = tile // CHUNK
    n_sub = tile // MIX_SUB
    cps = MIX_SUB // CHUNK
    halo = MIX_HALO

    @pl.when(i == 0)
    def _():
        state_ref[...] = jnp.zeros_like(state_ref)

    x_ext = jnp.concatenate([xp_ref[...], x_ref[...], xn_ref[...]], axis=0)
    h_ref[...] = _rmsnorm(x_ext, nm_ref[...]).astype(BF16)
    dt = _softplus(jnp.dot(h_ref[halo:halo + tile], win_ref[:, P_DT:P_DT + 128],
                           preferred_element_type=F32) + dtb_ref[...])
    dt_ref[...] = dt
    prep = _ssd_prep(dt, a_ref[...], reverse=True, n_chunks=n_chunks)
    cw = cw_ref[...]

    def project(s):
        lo = halo + s * MIX_SUB - (halo if s == 0 else 0)
        hi = halo + (s + 1) * MIX_SUB + (halo if s == n_sub - 1 else 0)
        proj = jnp.dot(h_ref[lo:hi], win_ref[:, 0:P_DT], preferred_element_type=F32)
        xe_ref[lo:hi, :] = proj[:, P_XBC:P_XBC + SSD_XBC]
        off = halo if s == 0 else 0
        pm = proj[off:off + MIX_SUB]
        rows = slice(s * MIX_SUB, (s + 1) * MIX_SUB)
        z_ref[rows, :] = pm[:, P_Z:P_Z + SSD_WIDTH]

        uv = _gelu(pm[:, P_SGU:P_SGU + 2 * SGU_WIDTH])
        u = uv[:, :SGU_WIDTH]
        v = uv[:, SGU_WIDTH:]
        mu = jnp.mean(v, axis=-1, keepdims=True)
        vc = v - mu
        var = jnp.mean(vc * vc, axis=-1, keepdims=True)
        v = vc * lax.rsqrt(var + 1e-5) * snw_ref[...] + snb_ref[...]
        mixed = []
        for c in range(cps):
            pairs = []
            for k in range(SGU_HEADS // 2):
                v_lo, v_hi = _split_pair(v[c * CHUNK:(c + 1) * CHUNK, k * CHUNK:(k + 1) * CHUNK])
                pairs.append(jnp.dot(sgw_ref[k], jnp.concatenate([v_lo, v_hi], axis=0),
                                     preferred_element_type=F32))
            mixed.append(jnp.concatenate(pairs, axis=1) + sgb_ref[...])
        mixed = jnp.concatenate(mixed, axis=0)
        ysgu_ref[rows, :] = _rmsnorm(u * mixed, son_ref[...]).astype(BF16)

        for c in range(cps):
            u5 = pm[c * CHUNK:(c + 1) * CHUNK, P_S5:P_S5 + S5_WIDTH]
            ut_ref[:, s * cps + c] = u5.T.astype(BF16).reshape(S5_GROUPS, S5_GROUP, CHUNK)

    def conv_ssd(s):
        base = halo + s * MIX_SUB
        prev = xe_ref[base - 1:base - 1 + MIX_SUB, :]
        cur = xe_ref[base:base + MIX_SUB, :]
        nxt = xe_ref[base + 1:base + 1 + MIX_SUB, :]
        r = lax.broadcasted_iota(jnp.int32, (MIX_SUB, 1), 0)
        if s == 0:
            prev = jnp.where(jnp.logical_and(r == 0, ti == 0), 0.0, prev)
        if s == n_sub - 1:
            nxt = jnp.where(jnp.logical_and(r == MIX_SUB - 1, ti == nt - 1), 0.0, nxt)
        xbc = _silu(cb_ref[...] + cw[0:1] * prev + cw[1:2] * cur + cw[2:3] * nxt)
        xbc_ref[s * MIX_SUB:(s + 1) * MIX_SUB, :] = xbc
        for c in reversed(range(cps)):
            sl = slice(c * CHUNK, (c + 1) * CHUNK)
            q = s * cps + c
            yb_ref[q * CHUNK:(q + 1) * CHUNK, :] = _ssd_chunk(
                q, xbc[sl, 0:512], xbc[sl, 512:768], xbc[sl, 768:1024], prep, state_ref, reverse=True)

    project(n_sub - 1)
    for s in reversed(range(n_sub)):
        if s > 0:
            project(s - 1)
        conv_ssd(s)


def _halo_maps(nt, tile, halo, n_tok, reverse):
    tpb = tile // halo

    def tidx(b, i):
        return b * nt + ((nt - 1 - i) if reverse else i)

    cur = lambda b, i: (tidx(b, i), 0)
    prev = lambda b, i: (jnp.maximum(tidx(b, i) * tpb - 1, 0), 0)
    nxt = lambda b, i: (jnp.minimum((tidx(b, i) + 1) * tpb, n_tok // halo - 1), 0)
    return cur, prev, nxt


def _const_spec(arr, layer=None):
    single = pl.Buffered(1)
    if layer is None:
        nd = arr.ndim
        return pl.BlockSpec(arr.shape, lambda b, i, _nd=nd: (0,) * _nd, pipeline_mode=single)
    nd = arr.ndim - 1
    return pl.BlockSpec((None,) + arr.shape[1:], lambda b, i, _nd=nd, _l=layer: (_l,) + (0,) * _nd,
                        pipeline_mode=single)


def _mixer_a(x2d, wts, layer, n_seq, seq_len):
    tile = MIX_TILE
    n_tok = n_seq * seq_len
    nt = seq_len // tile
    nc = tile // CHUNK
    cur, prev, nxt = _halo_maps(nt, tile, MIX_HALO, n_tok, reverse=True)
    tok = lambda w: pl.BlockSpec((tile, w), cur)
    ut_map = lambda b, i: (0, b * nt + (nt - 1 - i), 0, 0)
    params = [wts["norm_mix"], wts["w_in"], wts["ssd_conv_w"], wts["ssd_conv_b"], wts["dt_bias"],
              wts["ssd_a_log"], wts["sgu_w"], wts["sgu_b"], wts["sgu_norm_w"], wts["sgu_norm_b"],
              wts["sgu_out_norm"]]
    in_specs = [tok(D_MODEL), pl.BlockSpec((MIX_HALO, D_MODEL), prev),
                pl.BlockSpec((MIX_HALO, D_MODEL), nxt)]
    in_specs += [_const_spec(p, layer) for p in params]
    out_shape = [
        jax.ShapeDtypeStruct((n_tok, SSD_WIDTH), F32),
        jax.ShapeDtypeStruct((n_tok, SSD_XBC), F32),
        jax.ShapeDtypeStruct((n_tok, 128), F32),
        jax.ShapeDtypeStruct((n_tok, SSD_WIDTH), F32),
        jax.ShapeDtypeStruct((n_tok, SGU_WIDTH), BF16),
        jax.ShapeDtypeStruct((S5_GROUPS, n_tok // CHUNK, S5_GROUP, CHUNK), BF16),
    ]
    out_specs = [tok(SSD_WIDTH), tok(SSD_XBC), tok(128), tok(SSD_WIDTH), tok(SGU_WIDTH),
                 pl.BlockSpec((S5_GROUPS, nc, S5_GROUP, CHUNK), ut_map)]
    return pl.pallas_call(
        functools.partial(_mixer_a_kernel, tile=tile),
        grid=(n_seq, nt),
        in_specs=in_specs,
        out_specs=out_specs,
        out_shape=out_shape,
        scratch_shapes=[pltpu.VMEM((SSD_HEADS // 2, SSD_STATE, 2 * SSD_HEAD_DIM), F32),
                        pltpu.VMEM((tile + 2 * MIX_HALO, D_MODEL), BF16),
                        pltpu.VMEM((tile + 2 * MIX_HALO, SSD_XBC), F32)],
        compiler_params=pltpu.CompilerParams(
            dimension_semantics=("arbitrary", "arbitrary"), vmem_limit_bytes=VMEM_LIMIT_BYTES),
        name="mixer_a",
    )(x2d, x2d, x2d, *params)


def _cpow(base_re, base_im, n, n_bits):
    shape = n.shape
    res_re = jnp.ones(shape, F32)
    res_im = jnp.zeros(shape, F32)
    b_re = jnp.broadcast_to(base_re, shape)
    b_im = jnp.broadcast_to(base_im, shape)
    for bit in range(n_bits):
        on = ((n >> bit) & 1) == 1
        m_re = res_re * b_re - res_im * b_im
        m_im = res_re * b_im + res_im * b_re
        res_re = jnp.where(on, m_re, res_re)
        res_im = jnp.where(on, m_im, res_im)
        if bit + 1 < n_bits:
            b_re, b_im = b_re * b_re - b_im * b_im, 2.0 * b_re * b_im
    return res_re, res_im


def _s5_discretise(lr, li, st):
    mag = jnp.exp(lr * st)
    a_re = mag * jnp.cos(li * st)
    a_im = mag * jnp.sin(li * st)
    den = lr * lr + li * li
    f_re = ((a_re - 1.0) * lr + a_im * li) / den
    f_im = (a_im * lr - (a_re - 1.0) * li) / den
    return a_re, a_im, f_re, f_im


def _s5_gen_kernel(lrc_ref, lic_ref, lrr_ref, lir_ref, ls_ref, btr_ref, bti_ref, ctr_ref, cti_ref,
                   d_ref, sel_ref, mt_ref, woff_ref, a128_ref, k1_ref, k2_ref, bbar_ref, base_ref):
    m_lane = lax.broadcasted_iota(jnp.int32, (S5_STATE, CHUNK), 1)
    s_sub = lax.broadcasted_iota(jnp.int32, (CHUNK, S5_STATE), 0)
    ctr = ctr_ref[...]
    cti = cti_ref[...]

    def big_rhs(e_re, e_im):
        blocks = []
        for i in range(S5_GROUP):
            cr = ctr[:, i:i + 1]
            ci = cti[:, i:i + 1]
            blocks.append(jnp.concatenate([cr * e_re - ci * e_im, -(cr * e_im + ci * e_re)], axis=0))
        return jnp.concatenate(blocks, axis=1)

    a_row = []
    offs = []
    ks = []
    for k in range(2):
        st = jnp.exp(ls_ref[k])
        ac_re, ac_im, _, _ = _s5_discretise(lrc_ref[k], lic_ref[k], st)
        ar_re, ar_im, f_re, f_im = _s5_discretise(lrr_ref[k], lir_ref[k], st)
        bt_re = btr_ref[...]
        bt_im = bti_ref[...]
        bb_re = f_re * bt_re - f_im * bt_im
        bb_im = f_re * bt_im + f_im * bt_re
        for j in range(S5_GROUP):
            bbar_ref[2 * k, j] = bb_re[j:j + 1]
            bbar_ref[2 * k + 1, j] = bb_im[j:j + 1]
        a_row.append((ar_re, ar_im))
        if k == 0:
            n_k, n_off = m_lane, m_lane + 1
        else:
            n_k, n_off = (CHUNK - m_lane) & (CHUNK - 1), CHUNK - m_lane
        e_re, e_im = _cpow(ac_re, ac_im, n_k, 7)
        lhs = jnp.concatenate([bb_re, bb_im], axis=1)
        ks.append(jnp.dot(lhs, big_rhs(e_re, e_im), precision=HIGHEST, preferred_element_type=F32))
        o_re, o_im = _cpow(ac_re, ac_im, n_off, 8)
        offs.append(big_rhs(o_re, o_im))

    jj = lax.broadcasted_iota(jnp.int32, (S5_GROUP, S5_FLAT), 0)
    ll = lax.broadcasted_iota(jnp.int32, (S5_GROUP, S5_FLAT), 1)
    k_fwd = (ks[0] + jnp.where((ll & (CHUNK - 1)) == 0, ks[1], 0.0)
             + jnp.where(ll == jj * CHUNK, d_ref[...], 0.0))
    for j in range(S5_GROUP):
        k1_ref[j] = k_fwd[j:j + 1]
        k2_ref[j] = ks[1][j:j + 1]

    woff_ref[...] = jnp.concatenate(
        [offs[0][0:S5_STATE], offs[1][0:S5_STATE], offs[0][S5_STATE:], offs[1][S5_STATE:]],
        axis=0).astype(BF16)

    t128_f = _cpow(a_row[0][0], a_row[0][1], jnp.full((1, S5_STATE), CHUNK, jnp.int32), 8)
    t128_b = _cpow(a_row[1][0], a_row[1][1], jnp.full((1, S5_STATE), CHUNK, jnp.int32), 8)
    a128_ref[...] = jnp.concatenate([t128_f[0], t128_b[0], t128_f[1], t128_b[1]], axis=1)

    esf = _cpow(a_row[0][0], a_row[0][1], (CHUNK - 1) - s_sub, 7)
    esb = _cpow(a_row[1][0], a_row[1][1], s_sub, 7)

    def per_j(j, carry):
        row0 = pl.multiple_of(j * CHUNK, CHUNK)
        cols = []
        for k, (e_re, e_im) in enumerate((esf, esb)):
            b_re = bbar_ref[2 * k, j]
            b_im = bbar_ref[2 * k + 1, j]
            cols.append((e_re * b_re - e_im * b_im, e_re * b_im + e_im * b_re))
        mt_ref[pl.ds(row0, CHUNK), S5_FLAT:S5_MT_COLS] = jnp.concatenate(
            [cols[0][0], cols[1][0], cols[0][1], cols[1][1]], axis=1).astype(BF16)
        for i in range(S5_GROUP):
            w = jnp.concatenate([k2_ref[j, :, i * CHUNK:(i + 1) * CHUNK],
                                 k1_ref[j, :, i * CHUNK:(i + 1) * CHUNK]], axis=1)
            rolled = pltpu.roll(jnp.broadcast_to(w, (TOE_ROWS, 2 * CHUNK)), 0, axis=1,
                                stride=1, stride_axis=0)
            base_ref[i * TOE_ROWS:(i + 1) * TOE_ROWS, :] = rolled.astype(BF16)
        shifted = jnp.dot(base_ref[...], sel_ref[...], preferred_element_type=F32)
        for a in range(TOE_SHIFTS):
            for i in range(S5_GROUP):
                mt_ref[pl.ds(row0 + a * TOE_ROWS, TOE_ROWS), i * CHUNK:(i + 1) * CHUNK] = shifted[
                    i * TOE_ROWS:(i + 1) * TOE_ROWS, a * CHUNK:(a + 1) * CHUNK].astype(BF16)
        return carry

    lax.fori_loop(0, S5_GROUP, per_j, 0)


def _toeplitz_select():
    m = jnp.arange(2 * CHUNK)[:, None]
    a = jnp.arange(TOE_SHIFTS * CHUNK)[None, :] // CHUNK
    t = jnp.arange(TOE_SHIFTS * CHUNK)[None, :] % CHUNK
    return (m == t + CHUNK - TOE_ROWS * a).astype(BF16)


def _s5_gen(wts):
    lg = lambda *blk: pl.BlockSpec((None, 2, None) + blk, lambda l, g: (l, 0, g, 0, 0))
    per = lambda *blk: pl.BlockSpec((None, None) + blk, lambda l, g: (l, g, 0, 0))
    sel = _toeplitz_select()
    in_specs = [lg(S5_STATE, 1), lg(S5_STATE, 1), lg(1, S5_STATE), lg(1, S5_STATE), lg(1, 1),
                per(S5_GROUP, S5_STATE), per(S5_GROUP, S5_STATE),
                per(S5_STATE, S5_GROUP), per(S5_STATE, S5_GROUP), per(S5_GROUP, 1),
                pl.BlockSpec(sel.shape, lambda l, g: (0, 0))]
    out_shape = [jax.ShapeDtypeStruct((DEPTH, S5_GROUPS, S5_FLAT, S5_MT_COLS), BF16),
                 jax.ShapeDtypeStruct((DEPTH, S5_GROUPS, S5_NSTATE, S5_FLAT), BF16),
                 jax.ShapeDtypeStruct((DEPTH, S5_GROUPS, 1, S5_NSTATE), F32)]
    out_specs = [per(S5_FLAT, S5_MT_COLS), per(S5_NSTATE, S5_FLAT), per(1, S5_NSTATE)]
    return pl.pallas_call(
        _s5_gen_kernel,
        grid=(DEPTH, S5_GROUPS),
        in_specs=in_specs,
        out_specs=out_specs,
        out_shape=out_shape,
        scratch_shapes=[pltpu.VMEM((S5_GROUP, 1, S5_FLAT), F32), pltpu.VMEM((S5_GROUP, 1, S5_FLAT), F32),
                        pltpu.VMEM((4, S5_GROUP, 1, S5_STATE), F32),
                        pltpu.VMEM((S5_GROUP * TOE_ROWS, 2 * CHUNK), BF16)],
        compiler_params=pltpu.CompilerParams(
            dimension_semantics=("arbitrary", "arbitrary"), vmem_limit_bytes=VMEM_LIMIT_BYTES),
        name="s5_gen",
    )(wts["s5_lr_col"], wts["s5_li_col"], wts["s5_lr_row"], wts["s5_li_row"], wts["s5_ls"],
      wts["s5_bt_re"], wts["s5_bt_im"], wts["s5_ct_re"], wts["s5_ct_im"], wts["s5_d"], sel)


def _s5_stage_kernel(ut_ref, mt_ref, woff_ref, a128_ref, yt_ref, sre_ref, sim_ref,
                     fre_ref, fim_ref, bre_ref, bim_ref, *, n_seq, cps):
    z = jnp.dot(ut_ref[...], mt_ref[...], preferred_element_type=F32)
    sre_ref[...] = z[:, S5_FLAT:S5_FLAT + 128]
    sim_ref[...] = z[:, S5_FLAT + 128:S5_FLAT + 256]
    a_re = a128_ref[:, 0:128]
    a_im = a128_ref[:, 128:256]

    def rows(c):
        return pl.ds(c, n_seq, stride=cps) if n_seq > 1 else pl.ds(c, 1)

    def scan(c, carry, in_re_ref, in_im_ref):
        c_re, c_im = carry
        in_re_ref[rows(c), :] = c_re
        in_im_ref[rows(c), :] = c_im
        s_re = sre_ref[rows(c), :]
        s_im = sim_ref[rows(c), :]
        return (a_re * c_re - a_im * c_im + s_re, a_re * c_im + a_im * c_re + s_im)

    zero = (jnp.zeros((n_seq, 128), F32), jnp.zeros((n_seq, 128), F32))
    lax.fori_loop(0, cps, lambda c, cr: scan(c, cr, fre_ref, fim_ref), zero)
    lax.fori_loop(0, cps, lambda c, cr: scan(cps - 1 - c, cr, bre_ref, bim_ref), zero)

    n_rows = n_seq * cps
    lane = lax.broadcasted_iota(jnp.int32, (n_rows, 128), 1)
    fwd_lane = lane < S5_STATE
    sin = jnp.concatenate([jnp.where(fwd_lane, fre_ref[...], bre_ref[...]),
                           jnp.where(fwd_lane, fim_ref[...], bim_ref[...])], axis=1).astype(BF16)
    yt_ref[...] = z[:, 0:S5_FLAT] + jnp.dot(sin, woff_ref[...], preferred_element_type=F32)


def _s5_stage(ut, s5m, layer, n_seq, cps):
    mt, woff, a128 = s5m
    n_chunks = n_seq * cps
    ut3 = ut.reshape(S5_GROUPS, n_chunks, S5_FLAT)
    per = lambda *blk: pl.BlockSpec((None, None) + blk, lambda g, _l=layer: (_l, g, 0, 0))
    grp = pl.BlockSpec((None, n_chunks, S5_FLAT), lambda g: (g, 0, 0))
    scr = pltpu.VMEM((n_chunks, 128), F32)
    yt = pl.pallas_call(
        functools.partial(_s5_stage_kernel, n_seq=n_seq, cps=cps),
        grid=(S5_GROUPS,),
        in_specs=[grp, per(S5_FLAT, S5_MT_COLS), per(S5_NSTATE, S5_FLAT), per(1, S5_NSTATE)],
        out_specs=grp,
        out_shape=jax.ShapeDtypeStruct((S5_GROUPS, n_chunks, S5_FLAT), F32),
        scratch_shapes=[scr] * 6,
        compiler_params=pltpu.CompilerParams(
            dimension_semantics=("arbitrary",), vmem_limit_bytes=VMEM_LIMIT_BYTES),
        name="s5_stage",
    )(ut3, mt, woff, a128)
    return yt.reshape(S5_GROUPS, n_chunks, S5_GROUP, CHUNK)


def _mixer_c_kernel(x_ref, z_ref, xbc_ref, dt_ref, yb_ref, ysgu_ref, yt_ref,
                    a_ref, dx_ref, sn_ref, gw_ref, gb_ref, s5n_ref, wout_ref,
                    o_ref, state_ref, mix_ref, *, tile):
    i = pl.program_id(1)
    n_chunks = tile // CHUNK
    cps = MIX_SUB // CHUNK

    @pl.when(i == 0)
    def _():
        state_ref[...] = jnp.zeros_like(state_ref)

    prep = _ssd_prep(dt_ref[...], a_ref[...], reverse=False, n_chunks=n_chunks)
    for s in range(tile // MIX_SUB):
        for c in range(cps):
            q = s * cps + c
            sl = slice(q * CHUNK, (q + 1) * CHUNK)
            xs = xbc_ref[sl, 0:512]
            y = _ssd_chunk(q, xs, xbc_ref[sl, 512:768], xbc_ref[sl, 768:1024], prep, state_ref,
                           reverse=False)
            y = (y + yb_ref[sl, :] + dx_ref[...] * xs) * _silu(z_ref[sl, :])
            mix_ref[sl, 0:512] = _rmsnorm(y, sn_ref[...]).astype(BF16)

            yt = yt_ref[:, q].reshape(S5_WIDTH, CHUNK).T
            ys = _gelu(yt)
            gate = jnp.dot(ys.astype(BF16), gw_ref[...], preferred_element_type=F32) + gb_ref[...]
            ys = ys * jax.nn.sigmoid(gate)
            mix_ref[sl, 512:768] = _rmsnorm(ys, s5n_ref[...]).astype(BF16)
            mix_ref[sl, 768:1024] = ysgu_ref[sl, :]
        rows = slice(s * MIX_SUB, (s + 1) * MIX_SUB)
        o_ref[rows, :] = x_ref[rows, :] + jnp.dot(mix_ref[rows, :], wout_ref[...],
                                                  preferred_element_type=F32)


def _mixer_c(x2d, a_out, yt, wts, layer, n_seq, seq_len):
    tile = MIX_TILE
    z, xbc, dt, yb, ysgu = a_out
    n_tok = n_seq * seq_len
    nt = seq_len // tile
    nc = tile // CHUNK
    cur = lambda b, i: (b * nt + i, 0)
    tok = lambda w: pl.BlockSpec((tile, w), cur)
    params = [wts["ssd_a_log"], wts["ssd_d_x"], wts["ssd_norm"], wts["s5_glu_w"], wts["s5_glu_b"],
              wts["s5_out_norm"], wts["w_out"]]
    in_specs = [tok(D_MODEL), tok(SSD_WIDTH), tok(SSD_XBC), tok(128), tok(SSD_WIDTH), tok(SGU_WIDTH),
                pl.BlockSpec((S5_GROUPS, nc, S5_GROUP, CHUNK), lambda b, i: (0, b * nt + i, 0, 0))]
    in_specs += [_const_spec(p, layer) for p in params]
    return pl.pallas_call(
        functools.partial(_mixer_c_kernel, tile=tile),
        grid=(n_seq, nt),
        in_specs=in_specs,
        out_specs=tok(D_MODEL),
        out_shape=jax.ShapeDtypeStruct((n_tok, D_MODEL), F32),
        scratch_shapes=[pltpu.VMEM((SSD_HEADS // 2, SSD_STATE, 2 * SSD_HEAD_DIM), F32),
                        pltpu.VMEM((tile, D_MODEL), BF16)],
        compiler_params=pltpu.CompilerParams(
            dimension_semantics=("arbitrary", "arbitrary"), vmem_limit_bytes=VMEM_LIMIT_BYTES),
        name="mixer_c",
    )(x2d, z, xbc, dt, yb, ysgu, yt, *params)


def _ffn_kernel(x_ref, xp_ref, xn_ref, p_ref, nf_ref, wup_ref, cw_ref, cb_ref, wd_ref,
                pp_ref, pn_ref, pg_ref, fn_ref, o_ref, act_ref, *, tile, final):
    i = pl.program_id(1)
    nt = pl.num_programs(1)
    halo = FFN_HALO
    ext = tile + 2 * halo
    x = x_ref[...]
    x_ext = jnp.concatenate([xp_ref[...], x, xn_ref[...]], axis=0)
    hh = _rmsnorm(x_ext, nf_ref[...]).astype(BF16)
    r = lax.broadcasted_iota(jnp.int32, (tile, 1), 0)
    first_row = jnp.logical_and(r == 0, i == 0)
    last_row = jnp.logical_and(r == tile - 1, i == nt - 1)

    def up_conv(c0):
        cols = slice(c0, c0 + FF_BLOCK)
        up = jnp.dot(hh, wup_ref[:, cols], preferred_element_type=F32)
        cw = cw_ref[:, cols]
        prev = jnp.where(first_row, 0.0, pltpu.roll(up, 1, axis=0)[halo:halo + tile])
        nxt = jnp.where(last_row, 0.0, pltpu.roll(up, ext - 1, axis=0)[halo:halo + tile])
        return cb_ref[:, cols] + cw[0:1] * prev + cw[1:2] * up[halo:halo + tile] + cw[2:3] * nxt

    for j in range(N_FF_BLOCKS):
        gate = up_conv(j * FF_BLOCK)
        val = up_conv(D_FF + j * FF_BLOCK)
        act_ref[:, j * FF_BLOCK:(j + 1) * FF_BLOCK] = (_silu(gate) * val).astype(BF16)

    x2 = x + jnp.dot(act_ref[...], wd_ref[...], preferred_element_type=F32)
    e = jnp.dot(p_ref[...].astype(BF16), pp_ref[...], preferred_element_type=F32)
    g = jax.nn.sigmoid(jnp.dot(_rmsnorm(x2, pn_ref[...]).astype(BF16), pg_ref[...],
                               preferred_element_type=F32))
    x3 = x2 + g * e
    if final:
        x3 = _rmsnorm(x3, fn_ref[...])
    o_ref[...] = x3


def _ffn(x2d, p2d, wts, layer, n_seq, seq_len, final):
    tile = FFN_TILE
    n_tok = n_seq * seq_len
    nt = seq_len // tile
    cur, prev, nxt = _halo_maps(nt, tile, FFN_HALO, n_tok, reverse=False)
    params = [wts["norm_ffn"], wts["ffn_w_up"], wts["ffn_conv_w"], wts["ffn_conv_b"], wts["ffn_wd"],
              wts["ple_proj"], wts["ple_norm"], wts["ple_gate_w"]]
    in_specs = [pl.BlockSpec((tile, D_MODEL), cur), pl.BlockSpec((FFN_HALO, D_MODEL), prev),
                pl.BlockSpec((FFN_HALO, D_MODEL), nxt), pl.BlockSpec((tile, PLE_DIM), cur)]
    in_specs += [_const_spec(p, layer) for p in params]
    in_specs += [_const_spec(wts["final_norm"])]
    return pl.pallas_call(
        functools.partial(_ffn_kernel, tile=tile, final=final),
        grid=(n_seq, nt),
        in_specs=in_specs,
        out_specs=pl.BlockSpec((tile, D_MODEL), cur),
        out_shape=jax.ShapeDtypeStruct((n_tok, D_MODEL), F32),
        scratch_shapes=[pltpu.VMEM((tile, D_FF), BF16)],
        compiler_params=pltpu.CompilerParams(
            dimension_semantics=("arbitrary", "arbitrary"), vmem_limit_bytes=VMEM_LIMIT_BYTES),
        name="ffn",
    )(x2d, x2d, x2d, p2d, *params, wts["final_norm"])


def _prepare_weights(norm_mix, w_in, ssd_conv_w, ssd_conv_b, ssd_dt_bias, ssd_a_log, ssd_d, ssd_norm,
                     s5_lambda_re, s5_lambda_im, s5_log_step, s5_b_re, s5_b_im, s5_c_re, s5_c_im, s5_d,
                     s5_glu_w, s5_glu_b, s5_out_norm, sgu_norm_w, sgu_norm_b, sgu_w, sgu_b, sgu_out_norm,
                     w_out, norm_ffn, ffn_w_up, ffn_conv_w, ffn_conv_b, ffn_w_down,
                     ple_proj, ple_norm, ple_gate_w, final_norm):
    L = DEPTH
    row = lambda a: a.reshape(L, 1, a.shape[-1])
    n_dt = 2 * SSD_HEADS
    o_dt = 512 + SSD_XBC
    w_in_p = jnp.concatenate(
        [w_in[..., :o_dt], w_in[..., o_dt + n_dt:], w_in[..., o_dt:o_dt + n_dt],
         jnp.zeros((L, D_MODEL, P_WIDTH - IN_WIDTH), w_in.dtype)], axis=-1).astype(BF16)
    wts = {
        "norm_mix": row(norm_mix),
        "w_in": w_in_p,
        "ssd_conv_w": ssd_conv_w,
        "ssd_conv_b": row(ssd_conv_b),
        "dt_bias": jnp.pad(ssd_dt_bias.reshape(L, 1, n_dt), ((0, 0), (0, 0), (0, 128 - n_dt))),
        "ssd_a_log": jnp.tile(ssd_a_log.reshape(L, 1, n_dt), (1, 1, 128 // n_dt)),
        "ssd_d_x": jnp.repeat(ssd_d, SSD_HEAD_DIM, axis=-1).reshape(L, 1, SSD_WIDTH),
        "ssd_norm": row(ssd_norm),
        "s5_lr_col": s5_lambda_re[..., None],
        "s5_li_col": s5_lambda_im[..., None],
        "s5_lr_row": s5_lambda_re[..., None, :],
        "s5_li_row": s5_lambda_im[..., None, :],
        "s5_ls": s5_log_step[..., None, None],
        "s5_bt_re": jnp.swapaxes(s5_b_re, -1, -2),
        "s5_bt_im": jnp.swapaxes(s5_b_im, -1, -2),
        "s5_ct_re": jnp.swapaxes(s5_c_re, -1, -2),
        "s5_ct_im": jnp.swapaxes(s5_c_im, -1, -2),
        "s5_d": s5_d.reshape(L, S5_GROUPS, S5_GROUP, 1),
        "s5_glu_w": s5_glu_w.astype(BF16),
        "s5_glu_b": row(s5_glu_b),
        "s5_out_norm": row(s5_out_norm),
        "sgu_norm_w": row(sgu_norm_w),
        "sgu_norm_b": row(sgu_norm_b),
        "sgu_w": jnp.concatenate([sgu_w[:, 0::2], sgu_w[:, 1::2]], axis=-1).astype(BF16),
        "sgu_b": jnp.repeat(jnp.swapaxes(sgu_b, -1, -2), SGU_HEAD_DIM, axis=-1),
        "sgu_out_norm": row(sgu_out_norm),
        "w_out": w_out.astype(BF16),
        "norm_ffn": row(norm_ffn),
        "ffn_w_up": ffn_w_up.astype(BF16),
        "ffn_conv_w": ffn_conv_w,
        "ffn_conv_b": row(ffn_conv_b),
        "ffn_wd": ffn_w_down.astype(BF16),
        "ple_proj": ple_proj.astype(BF16),
        "ple_norm": row(ple_norm),
        "ple_gate_w": ple_gate_w.astype(BF16),
        "final_norm": final_norm.reshape(1, D_MODEL),
    }
    return wts


def _encoder(x, p, wts, s5m):
    n_seq, seq_len, _ = x.shape
    n_tok = n_seq * seq_len
    x2d = x.reshape(n_tok, D_MODEL)
    for layer in range(DEPTH):
        *a_out, ut = _mixer_a(x2d, wts, layer, n_seq, seq_len)
        yt = _s5_stage(ut, s5m, layer, n_seq, seq_len // CHUNK)
        x2d = _mixer_c(x2d, a_out, yt, wts, layer, n_seq, seq_len)
        x2d = _ffn(x2d, p[layer].reshape(n_tok, PLE_DIM), wts, layer, n_seq, seq_len,
                   final=(layer == DEPTH - 1))
    return x2d.reshape(n_seq, seq_len, D_MODEL)


def kernel(x_prompt, x_sample, p_prompt, p_sample, norm_mix, w_in, ssd_conv_w, ssd_conv_b, ssd_dt_bias, ssd_a_log, ssd_d, ssd_norm, s5_lambda_re, s5_lambda_im, s5_log_step, s5_b_re, s5_b_im, s5_c_re, s5_c_im, s5_d, s5_glu_w, s5_glu_b, s5_out_norm, sgu_norm_w, sgu_norm_b, sgu_w, sgu_b, sgu_out_norm, w_out, norm_ffn, ffn_w_up, ffn_conv_w, ffn_conv_b, ffn_w_down, ple_proj, ple_norm, ple_gate_w, final_norm):
    wts = _prepare_weights(norm_mix, w_in, ssd_conv_w, ssd_conv_b, ssd_dt_bias, ssd_a_log, ssd_d, ssd_norm,
                           s5_lambda_re, s5_lambda_im, s5_log_step, s5_b_re, s5_b_im, s5_c_re, s5_c_im,
                           s5_d, s5_glu_w, s5_glu_b, s5_out_norm, sgu_norm_w, sgu_norm_b, sgu_w, sgu_b,
                           sgu_out_norm, w_out, norm_ffn, ffn_w_up, ffn_conv_w, ffn_conv_b, ffn_w_down,
                           ple_proj, ple_norm, ple_gate_w, final_norm)
    s5m = _s5_gen(wts)
    y_prompt = _encoder(x_prompt, p_prompt, wts, s5m)
    y_sample = _encoder(x_sample, p_sample, wts, s5m)
    return (y_prompt, y_sample)
```
